```python
import jax
import jax.numpy as jnp
from jax import lax
import numpy as np

D_MODEL = 1024
BATCH = 4
SEQ = 8192
DEPTH = 1
DEC_BATCH = 16
DEC_SEQ = 4096
PAST_LEN = 128

MIX_WIDTH = D_MODEL
POOL_WIDTH = MIX_WIDTH // 2
HG_WIDTH = MIX_WIDTH - POOL_WIDTH
POOL_WINDOWS = (2, 4, 8, 16)
POOL_GROUPS = len(POOL_WINDOWS)
POOL_GROUP_DIM = POOL_WIDTH // POOL_GROUPS
HG_EXPAND = 128
HG_HEADS = HG_WIDTH // HG_EXPAND
HG_DK = HG_EXPAND
HG_DV = HG_WIDTH // HG_HEADS
CHUNK = 16
IN_COLS = POOL_WIDTH + 5 * HG_WIDTH
N_GROUPS = 4
EXPERTS_PER_GROUP = 8
N_EXPERTS = N_GROUPS * EXPERTS_PER_GROUP
TOP_K = 2
D_EXPERT = D_MODEL // 2
EXPERT_BLOCK = 128
EPS = 1e-6

kernel_name = 'hybrid_pool_hgrn2_hiermoe_encoder'


def rmsnorm(x, gain):
    xf = x.astype(jnp.float32)
    y = xf * lax.rsqrt(jnp.mean(xf * xf, axis=-1, keepdims=True) + EPS)
    return (y * gain.astype(jnp.float32)).astype(x.dtype)


def multiscale_pool(u, pool_w, pool_scale):
    B, S, _ = u.shape
    ug = u.astype(jnp.float32).reshape(B, S, POOL_GROUPS, POOL_GROUP_DIM)
    cs = jnp.concatenate([jnp.zeros((B, 1, POOL_GROUPS, POOL_GROUP_DIM), jnp.float32),
                          jnp.cumsum(ug, axis=1)], axis=1)
    t = jnp.arange(S)
    groups = []
    for gi, w in enumerate(POOL_WINDOWS):
        lo = jnp.clip(t - w // 2, 0, S)
        hi = jnp.clip(t + w // 2, 0, S)
        cs_g = cs[:, :, gi]
        window_sum = jnp.take(cs_g, hi, axis=1) - jnp.take(cs_g, lo, axis=1)
        count = (hi - lo).astype(jnp.float32)[None, :, None]
        groups.append(window_sum / count - ug[:, :, gi])
    pooled = jnp.stack(groups, axis=2)
    y = jnp.einsum('bsgc,gcd->bsgd', pooled, pool_w.astype(jnp.float32))
    y = y.reshape(B, S, POOL_WIDTH) * pool_scale.astype(jnp.float32)
    return y.astype(u.dtype)


def chunked_gated_recurrence(q, k, v, log_f):
    B, S, H, DK = q.shape
    DV = v.shape[-1]
    N = S // CHUNK

    def chunks(a):
        return a.reshape(B, N, CHUNK, H, a.shape[-1]).transpose(0, 3, 1, 2, 4)

    q, k, v, log_f = chunks(q), chunks(k), chunks(v), chunks(log_f)
    b = jnp.cumsum(log_f, axis=3)
    b_last = b[:, :, :, -1:, :]
    q_dec = q * jnp.exp(b)
    k_intra = k * jnp.exp(-b)
    k_state = k * jnp.exp(b_last - b)
    chunk_decay = jnp.exp(b_last[:, :, :, 0, :])
    tri_mask = jnp.tril(jnp.ones((CHUNK, CHUNK), dtype=bool))
    scores = jnp.einsum('bhncd,bhnsd->bhncs', q_dec, k_intra)
    scores = jnp.where(tri_mask, scores, 0.0)
    o_intra = jnp.einsum('bhncs,bhnse->bhnce', scores, v)

    def step(state, xs):
        q_c, k_c, v_c, dec_c = xs
        o_c = jnp.einsum('bhcd,bhde->bhce', q_c, state)
        state = dec_c[..., None] * state + jnp.einsum('bhcd,bhce->bhde', k_c, v_c)
        return state, o_c

    xs = (jnp.moveaxis(q_dec, 2, 0), jnp.moveaxis(k_state, 2, 0),
          jnp.moveaxis(v, 2, 0), jnp.moveaxis(chunk_decay, 2, 0))
    _, o_inter = lax.scan(step, jnp.zeros((B, H, DK, DV), jnp.float32), xs)
    o = o_intra + jnp.moveaxis(o_inter, 0, 2)
    return o.transpose(0, 2, 3, 1, 4).reshape(B, S, H, DV)


def hgrn2_mixer(u_q, u_ff, u_fb, u_i, u_g, lb_fwd, lb_bwd, hg_norm_gain):
    B, S, _ = u_q.shape

    def heads(a):
        return a.astype(jnp.float32).reshape(B, S, HG_HEADS, -1)

    q = jax.nn.silu(heads(u_q))
    v = heads(u_i)

    def gate(logit, lb):
        f = lb + (1.0 - lb) * jax.nn.sigmoid(logit.astype(jnp.float32))
        return heads(1.0 - f), heads(jnp.log(f))

    k_f, g_f = gate(u_ff, lb_fwd)
    k_b, g_b = gate(u_fb, lb_bwd)
    o_fwd = chunked_gated_recurrence(q, k_f, v, g_f)

    def rev(a):
        return jnp.flip(a, axis=1)

    o_bwd = rev(chunked_gated_recurrence(rev(q), rev(k_b), rev(v), rev(g_b)))
    o = o_fwd + o_bwd
    o = o * lax.rsqrt(jnp.mean(o * o, axis=-1, keepdims=True) + EPS) * hg_norm_gain.astype(jnp.float32)
    o = o.reshape(B, S, HG_WIDTH) * jax.nn.silu(u_g.astype(jnp.float32))
    return o.astype(u_q.dtype)


def hierarchical_moe(x, router_group_w, router_group_b, router_expert_w, router_expert_b, w1, w3, w2):
    B, S, D = x.shape
    T = B * S
    A = T * TOP_K
    xt = x.reshape(T, D)
    group_logits = (jnp.einsum('td,dg->tg', xt, router_group_w).astype(jnp.float32)
                    + router_group_b.astype(jnp.float32))
    group_prob = jax.nn.softmax(group_logits, axis=-1)
    _, grp = lax.top_k(group_logits, 1)
    grp_prob = jnp.take_along_axis(group_prob, grp, axis=1)
    expert_logits = (jnp.einsum('td,de->te', xt, router_expert_w).astype(jnp.float32)
                     + router_expert_b.astype(jnp.float32)).reshape(T, N_GROUPS, EXPERTS_PER_GROUP)
    in_group = jnp.take_along_axis(expert_logits, grp[:, :, None], axis=1)[:, 0]
    top_val, top_idx = lax.top_k(in_group, TOP_K)
    gate = jax.nn.softmax(top_val, axis=-1) * grp_prob
    expert_id = grp * EXPERTS_PER_GROUP + top_idx

    flat_e = expert_id.reshape(A).astype(jnp.int32)
    order = jnp.argsort(flat_e)
    sorted_e = flat_e[order]
    tok_sorted = (order // TOP_K).astype(jnp.int32)
    gate_sorted = gate.reshape(A)[order]
    counts = jnp.zeros((N_EXPERTS,), jnp.int32).at[flat_e].add(1)
    start = jnp.cumsum(counts) - counts
    padded = (counts + EXPERT_BLOCK - 1) // EXPERT_BLOCK * EXPERT_BLOCK
    pend = jnp.cumsum(padded)
    pstart = pend - padded
    dest = pstart[sorted_e] + jnp.arange(A, dtype=jnp.int32) - start[sorted_e]
    n_blocks = -(-A // EXPERT_BLOCK) + N_EXPERTS
    slot_tok = jnp.full((n_blocks * EXPERT_BLOCK,), T, jnp.int32).at[dest].set(tok_sorted)
    block_expert = jnp.minimum(
        jnp.searchsorted(pend, jnp.arange(n_blocks, dtype=jnp.int32) * EXPERT_BLOCK, side='right'),
        N_EXPERTS - 1)
    x_pad = jnp.concatenate([xt, jnp.zeros((1, D), xt.dtype)], axis=0)

    def expert_block(args):
        toks, e = args
        xb = x_pad[toks]
        h = jax.nn.silu(xb @ w1[e]) * (xb @ w3[e])
        return h @ w2[e]

    y_blocks = lax.map(expert_block, (slot_tok.reshape(n_blocks, EXPERT_BLOCK), block_expert))
    y_sorted = y_blocks.reshape(n_blocks * EXPERT_BLOCK, D)[dest]
    y = jax.ops.segment_sum(y_sorted * gate_sorted[:, None].astype(x.dtype), tok_sorted, num_segments=T)
    return y.reshape(B, S, D).astype(x.dtype)


def layer_lower_bounds(param):
    return jnp.cumsum(jax.nn.softmax(param.astype(jnp.float32), axis=0), axis=0)


def encoder_layer(x, w_in, w_out, pool_w, pool_scale, lb_fwd, lb_bwd, hg_norm_gain, norm_mix, norm_ffn,
                  router_group_w, router_group_b, router_expert_w, router_expert_b, w1, w3, w2):
    n = rmsnorm(x, norm_mix)
    u = jnp.einsum('bsd,de->bse', n, w_in)
    cuts = [POOL_WIDTH + j * HG_WIDTH for j in range(5)]
    u_pool, u_q, u_ff, u_fb, u_i, u_g = jnp.split(u, cuts, axis=-1)
    mix = jnp.concatenate([multiscale_pool(u_pool, pool_w, pool_scale),
                           hgrn2_mixer(u_q, u_ff, u_fb, u_i, u_g, lb_fwd, lb_bwd, hg_norm_gain)], axis=-1)
    x = x + jnp.einsum('bse,ed->bsd', mix, w_out)
    x = x + hierarchical_moe(rmsnorm(x, norm_ffn), router_group_w, router_group_b,
                             router_expert_w, router_expert_b, w1, w3, w2)
    return x


def trunk(x, w_in, w_out, pool_w, pool_scale, hg_lb_fwd, hg_lb_bwd, hg_norm_gain, norm_mix, norm_ffn,
          router_group_w, router_group_b, router_expert_w, router_expert_b,
          expert_w1, expert_w3, expert_w2, norm_final):
    lb_f = layer_lower_bounds(hg_lb_fwd)
    lb_b = layer_lower_bounds(hg_lb_bwd)
    for l in range(DEPTH):
        x = encoder_layer(x, w_in[l], w_out[l], pool_w[l], pool_scale[l], lb_f[l], lb_b[l], hg_norm_gain[l],
                          norm_mix[l], norm_ffn[l], router_group_w[l], router_group_b[l],
                          router_expert_w[l], router_expert_b[l], expert_w1[l], expert_w3[l], expert_w2[l])
    return rmsnorm(x, norm_final)


def setup_inputs(seed: int = 0) -> dict:
    key = jax.random.key(seed)
    ks = jax.random.split(key, 20)

    def nrm(k, shape, scale):
        return jax.random.normal(k, shape, jnp.float32) * scale

    return {
        'x_prompt': nrm(ks[0], (BATCH, SEQ, D_MODEL), 1.0),
        'x_sample': nrm(ks[1], (DEC_BATCH, DEC_SEQ, D_MODEL), 1.0),
        'w_in': nrm(ks[2], (DEPTH, D_MODEL, IN_COLS), D_MODEL ** -0.5),
        'w_out': nrm(ks[3], (DEPTH, MIX_WIDTH, D_MODEL), MIX_WIDTH ** -0.5),
        'pool_w': nrm(ks[4], (DEPTH, POOL_GROUPS, POOL_GROUP_DIM, POOL_GROUP_DIM), POOL_GROUP_DIM ** -0.5),
        'pool_scale': 1.0 + nrm(ks[5], (DEPTH, POOL_WIDTH), 0.02),
        'hg_lb_fwd': nrm(ks[6], (DEPTH + 1, HG_WIDTH), 0.5),
        'hg_lb_bwd': nrm(ks[7], (DEPTH + 1, HG_WIDTH), 0.5),
        'hg_norm_gain': 1.0 + nrm(ks[8], (DEPTH, HG_DV), 0.02),
        'norm_mix': 1.0 + nrm(ks[9], (DEPTH, D_MODEL), 0.02),
        'norm_ffn': 1.0 + nrm(ks[10], (DEPTH, D_MODEL), 0.02),
        'router_group_w': nrm(ks[11], (DEPTH, D_MODEL, N_GROUPS), D_MODEL ** -0.5),
        'router_group_b': nrm(ks[12], (DEPTH, N_GROUPS), 0.01),
        'router_expert_w': nrm(ks[13], (DEPTH, D_MODEL, N_EXPERTS), D_MODEL ** -0.5),
        'router_expert_b': nrm(ks[14], (DEPTH, N_EXPERTS), 0.01),
        'expert_w1': nrm(ks[15], (DEPTH, N_EXPERTS, D_MODEL, D_EXPERT), D_MODEL ** -0.5),
        'expert_w3': nrm(ks[16], (DEPTH, N_EXPERTS, D_MODEL, D_EXPERT), D_MODEL ** -0.5),
        'expert_w2': nrm(ks[17], (DEPTH, N_EXPERTS, D_EXPERT, D_MODEL), D_EXPERT ** -0.5),
        'norm_final': 1.0 + nrm(ks[18], (D_MODEL,), 0.02),
    }


def reference(x_prompt, x_sample, w_in, w_out, pool_w, pool_scale, hg_lb_fwd, hg_lb_bwd, hg_norm_gain,
              norm_mix, norm_ffn, router_group_w, router_group_b, router_expert_w, router_expert_b,
              expert_w1, expert_w3, expert_w2, norm_final):
    y_prompt = trunk(x_prompt, w_in, w_out, pool_w, pool_scale, hg_lb_fwd, hg_lb_bwd, hg_norm_gain,
                     norm_mix, norm_ffn, router_group_w, router_group_b, router_expert_w, router_expert_b,
                     expert_w1, expert_w3, expert_w2, norm_final)
    y_sample = trunk(x_sample, w_in, w_out, pool_w, pool_scale, hg_lb_fwd, hg_lb_bwd, hg_norm_gain,
                     norm_mix, norm_ffn, router_group_w, router_group_b, router_expert_w, router_expert_b,
                     expert_w1, expert_w3, expert_w2, norm_final)
    return (y_prompt, y_sample)
```

```python
import functools

import jax
import jax.numpy as jnp
import numpy as np
from jax import lax
from jax.experimental import pallas as pl
from jax.experimental.pallas import tpu as pltpu

F32 = jnp.float32
BF16 = jnp.bfloat16

EPS = 1e-6
POOL_WINDOWS = (2, 4, 8, 16)
TOP_K = 2

LANE = 128
SUBLANE = 8
VMEM_LIMIT = 56 * 1024 * 1024

HG_CHUNK = 32
HG_UNROLL = 4
POOL_ROWS = 128
EXPERT_ROWS = 256


def _params(n_axes):
    return pltpu.CompilerParams(dimension_semantics=("arbitrary",) * n_axes, vmem_limit_bytes=VMEM_LIMIT)


def _rms(x, gain):
    return x * lax.rsqrt(jnp.mean(x * x, axis=-1, keepdims=True) + EPS) * gain


def _inproj_body(x_ref, gain_ref, w_ref, u_ref, *, nc):
    n = _rms(x_ref[0], gain_ref[...]).astype(BF16)
    per = nc // LANE
    for c in range(w_ref.shape[1] // nc):
        r = jnp.dot(n, w_ref[:, c * nc:(c + 1) * nc], preferred_element_type=F32)
        for j in range(per):
            u_ref[0, c * per + j] = r[:, j * LANE:(j + 1) * LANE].astype(BF16)


def _inproj(x, gain, w, tm):
    B, S, D = x.shape
    cols = w.shape[1]
    return pl.pallas_call(
        functools.partial(_inproj_body, nc=4 * LANE),
        grid=(B, S // tm),
        in_specs=[
            pl.BlockSpec((1, tm, D), lambda b, i: (b, i, 0)),
            pl.BlockSpec((1, D), lambda b, i: (0, 0)),
            pl.BlockSpec((D, cols), lambda b, i: (0, 0)),
        ],
        out_specs=pl.BlockSpec((1, cols // LANE, tm, LANE), lambda b, i: (b, 0, i, 0)),
        out_shape=jax.ShapeDtypeStruct((B, cols // LANE, S, LANE), BF16),
        compiler_params=_params(2),
        name="inproj",
    )(x, gain, w)


def _band_matrices(rows):
    t = np.arange(rows)[:, None]
    s = np.arange(rows)[None, :]
    out = np.zeros((len(POOL_WINDOWS), 3, rows, rows), np.float32)
    for gi, w in enumerate(POOL_WINDOWS):
        for k, shift in enumerate((-rows, 0, rows)):
            pos = s + shift
            out[gi, k] = (pos >= t - w // 2) & (pos < t + w // 2)
    return out


def _pool_body(half_ref, u_ref, band_ref, pw_ref, sc_ref, o_ref, *, seq, rows):
    nt = seq // rows
    h = half_ref[pl.program_id(1)]

    def tile(i, carry):
        r0 = pl.multiple_of(i * rows, rows)
        rp = pl.multiple_of(jnp.maximum(i - 1, 0) * rows, rows)
        rn = pl.multiple_of(jnp.minimum(i + 1, nt - 1) * rows, rows)
        xc = u_ref[0, 0, pl.ds(r0, rows), :]
        s = jnp.dot(band_ref[0, 1], xc, preferred_element_type=F32)
        sp = jnp.dot(band_ref[0, 0], u_ref[0, 0, pl.ds(rp, rows), :], preferred_element_type=F32)
        sn = jnp.dot(band_ref[0, 2], u_ref[0, 0, pl.ds(rn, rows), :], preferred_element_type=F32)
        s = s + jnp.where(i > 0, sp, 0.0) + jnp.where(i < nt - 1, sn, 0.0)
        t = r0 + lax.broadcasted_iota(jnp.int32, (rows, LANE), 0)
        cnt = (jnp.minimum(t + h, seq) - jnp.maximum(t - h, 0)).astype(F32)
        pooled = s / cnt - xc.astype(F32)
        y = jnp.dot(pooled.astype(BF16), pw_ref[0], preferred_element_type=F32) * sc_ref[0]
        o_ref[0, 0, pl.ds(r0, rows), :] = y.astype(BF16)
        return carry

    lax.fori_loop(0, nt, tile, 0)


def _pool(u, pool_w, pool_scale):
    B, _, S, _ = u.shape
    G = pool_w.shape[0]
    rows = min(POOL_ROWS, S)
    band = jnp.asarray(_band_matrices(rows), BF16)
    halves = jnp.asarray([w // 2 for w in POOL_WINDOWS], jnp.int32)
    grid_spec = pltpu.PrefetchScalarGridSpec(
        num_scalar_prefetch=1,
        grid=(B, G),
        in_specs=[
            pl.BlockSpec((1, 1, S, LANE), lambda b, g, h: (b, g, 0, 0)),
            pl.BlockSpec((1, 3, rows, rows), lambda b, g, h: (g, 0, 0, 0)),
            pl.BlockSpec((1, LANE, LANE), lambda b, g, h: (g, 0, 0)),
            pl.BlockSpec((1, 1, LANE), lambda b, g, h: (g, 0, 0)),
        ],
        out_specs=pl.BlockSpec((1, 1, S, LANE), lambda b, g, h: (b, g, 0, 0)),
    )
    return pl.pallas_call(
        functools.partial(_pool_body, seq=S, rows=rows),
        grid_spec=grid_spec,
        out_shape=jax.ShapeDtypeStruct((B, G, S, LANE), BF16),
        compiler_params=_params(2),
        name="pool",
    )(halves, u, band, pool_w.astype(BF16), pool_scale.reshape(G, 1, LANE).astype(F32))


def _hg_chunk(q_ref, f_ref, v_ref, r0, lb, tri, mask, st, *, reverse):
    C = HG_CHUNK
    rows = pl.ds(r0, C)
    q = q_ref[0, 0, rows, :].astype(F32)
    q = q * jax.nn.sigmoid(q)
    v = v_ref[0, 0, rows, :]
    f = lb + (1.0 - lb) * jax.nn.sigmoid(f_ref[0, 0, rows, :].astype(F32))
    k = 1.0 - f
    g = jnp.log(f)
    g_hi = g.astype(BF16)
    g_lo = (g - g_hi.astype(F32)).astype(BF16)
    b = jnp.dot(tri, g_hi, preferred_element_type=F32) + jnp.dot(tri, g_lo, preferred_element_type=F32)
    if reverse:
        b_end, b_mid = b[0:1], b[C // 2:C // 2 + 1]
    else:
        b_end, b_mid = b[C - 1:C], b[C // 2 - 1:C // 2]
    q_dec = (q * jnp.exp(b)).astype(BF16)
    q_in = (q * jnp.exp(b - b_mid)).astype(BF16)
    k_in = (k * jnp.exp(b_mid - b)).astype(BF16)
    k_st = (k * jnp.exp(b_end - b)).astype(BF16)
    contract_last = (((1,), (1,)), ((), ()))
    scores = lax.dot_general(q_in, k_in, contract_last, preferred_element_type=F32)
    scores = jnp.where(mask, scores, 0.0).astype(BF16)
    o = jnp.dot(scores, v, preferred_element_type=F32)
    o = o + lax.dot_general(q_dec, st.astype(BF16), contract_last, preferred_element_type=F32)
    kv_t = lax.dot_general(v, k_st, (((0,), (0,)), ((), ())), preferred_element_type=F32)
    return o, st * jnp.exp(b_end) + kv_t


def _hgrn_body(q_ref, ff_ref, fb_ref, v_ref, g_ref, lbf_ref, lbb_ref, gain_ref, tri_ref,
               o_ref, of_scr, ob_scr, *, seq, rows):
    C = HG_CHUNK
    n_chunks = seq // C
    lb_f = lbf_ref[0]
    lb_b = lbb_ref[0]
    low = tri_ref[0]
    up = tri_ref[1]
    ri = lax.broadcasted_iota(jnp.int32, (C, C), 0)
    ci = lax.broadcasted_iota(jnp.int32, (C, C), 1)
    mask_f = ci <= ri
    mask_b = ci >= ri

    def step(j, carry):
        st_f, st_b = carry
        rf = pl.multiple_of(j * C, C)
        rb = pl.multiple_of((n_chunks - 1 - j) * C, C)
        o_f, st_f = _hg_chunk(q_ref, ff_ref, v_ref, rf, lb_f, low, mask_f, st_f, reverse=False)
        o_b, st_b = _hg_chunk(q_ref, fb_ref, v_ref, rb, lb_b, up, mask_b, st_b, reverse=True)
        of_scr[pl.ds(rf, C), :] = o_f
        ob_scr[pl.ds(rb, C), :] = o_b
        return st_f, st_b

    zero = jnp.zeros((LANE, LANE), F32)
    lax.fori_loop(0, n_chunks, step, (zero, zero), unroll=min(HG_UNROLL, n_chunks))

    def finish(i, carry):
        r = pl.ds(pl.multiple_of(i * rows, rows), rows)
        o = of_scr[r, :] + ob_scr[r, :]
        o = _rms(o, gain_ref[...])
        gate = g_ref[0, 0, r, :].astype(F32)
        o_ref[0, 0, r, :] = (o * (gate * jax.nn.sigmoid(gate))).astype(BF16)
        return carry

    lax.fori_loop(0, seq // rows, finish, 0)


def _hgrn(u, lb_f, lb_b, gain, n_pool):
    B, _, S, _ = u.shape
    H = lb_f.shape[0]
    C = HG_CHUNK
    low = np.tril(np.ones((C, C), np.float32))
    tri = jnp.asarray(np.stack([low, low.T]), BF16)
    rows = min(256, S)

    def slab(k):
        return pl.BlockSpec((1, 1, S, LANE), lambda b, h, k=k: (b, n_pool + k * H + h, 0, 0))

    head_vec = pl.BlockSpec((1, 1, LANE), lambda b, h: (h, 0, 0))
    return pl.pallas_call(
        functools.partial(_hgrn_body, seq=S, rows=rows),
        grid=(B, H),
        in_specs=[slab(0), slab(1), slab(2), slab(3), slab(4), head_vec, head_vec,
                  pl.BlockSpec((1, LANE), lambda b, h: (0, 0)),
                  pl.BlockSpec((2, C, C), lambda b, h: (0, 0, 0))],
        out_specs=pl.BlockSpec((1, 1, S, LANE), lambda b, h: (b, h, 0, 0)),
        out_shape=jax.ShapeDtypeStruct((B, H, S, LANE), BF16),
        scratch_shapes=[pltpu.VMEM((S, LANE), F32), pltpu.VMEM((S, LANE), F32)],
        compiler_params=_params(2),
        name="hgrn",
    )(u, u, u, u, u, lb_f, lb_b, gain, tri)


INFO_GATE = 0
INFO_EXPERT = 2
INFO_RANK = 4


def _outproj_body(mp_ref, mh_ref, x_ref, wo_ref, gain_ref, wrh_ref, wrl_ref, rb_ref, ls_ref,
                  x2_ref, xn_ref, info_ref, cnt_ref, *, n_groups, epg):
    tm = x_ref.shape[1]

    @pl.when((pl.program_id(0) == 0) & (pl.program_id(1) == 0))
    def _():
        cnt_ref[...] = jnp.zeros_like(cnt_ref)

    mix = jnp.concatenate([mp_ref[0, j] for j in range(mp_ref.shape[1])]
                          + [mh_ref[0, j] for j in range(mh_ref.shape[1])], axis=-1)
    x2 = x_ref[0] + jnp.dot(mix, wo_ref[...], preferred_element_type=F32)
    x2_ref[0] = x2
    xn = _rms(x2, gain_ref[...])
    for j in range(xn.shape[1] // LANE):
        xn_ref[pl.ds(j, tm, stride=xn.shape[1] // LANE), :] = xn[:, j * LANE:(j + 1) * LANE]

    xh = xn.astype(BF16)
    xl = (xn - xh.astype(F32)).astype(BF16)
    logits = (jnp.dot(xh, wrh_ref[...], preferred_element_type=F32)
              + jnp.dot(xl, wrh_ref[...], preferred_element_type=F32)
              + jnp.dot(xh, wrl_ref[...], preferred_element_type=F32)) + rb_ref[...]

    lane = lax.broadcasted_iota(jnp.int32, (tm, LANE), 1)
    neg = jnp.float32(-jnp.inf)
    big = jnp.int32(LANE)

    def top(vals):
        m = jnp.max(vals, axis=-1, keepdims=True)
        return m, jnp.min(jnp.where(vals == m, lane, big), axis=-1, keepdims=True)

    glog = jnp.where(lane < n_groups, logits, neg)
    gmax, grp = top(glog)
    grp_prob = 1.0 / jnp.sum(jnp.exp(glog - gmax), axis=-1, keepdims=True)
    e_lo = n_groups + grp * epg
    elog = jnp.where((lane >= e_lo) & (lane < e_lo + epg), logits, neg)
    v1, i1 = top(elog)
    v2, i2 = top(jnp.where(lane == i1, neg, elog))
    e21 = jnp.exp(v2 - v1)
    gate1 = grp_prob / (1.0 + e21)
    gate2 = grp_prob * e21 / (1.0 + e21)

    hot1 = lane == i1
    hot2 = lane == i2
    onehot = jnp.where(hot1 | hot2, 1.0, 0.0)
    before = jnp.dot(ls_ref[...], onehot.astype(BF16), preferred_element_type=F32) + cnt_ref[...]
    rank1 = jnp.sum(jnp.where(hot1, before, 0.0), axis=-1, keepdims=True)
    rank2 = jnp.sum(jnp.where(hot2, before, 0.0), axis=-1, keepdims=True)
    cnt_ref[...] = cnt_ref[...] + jnp.sum(onehot, axis=0, keepdims=True)

    info = jnp.zeros((tm, LANE), F32)
    for k, col in ((INFO_GATE, gate1), (INFO_GATE + 1, gate2),
                   (INFO_EXPERT, (i1 - n_groups).astype(F32)), (INFO_EXPERT + 1, (i2 - n_groups).astype(F32)),
                   (INFO_RANK, rank1), (INFO_RANK + 1, rank2)):
        info = jnp.where(lane == k, col, info)
    info_ref[0] = info


def _outproj(mixp, mixh, x, w_out, gain, wr_hi, wr_lo, rbias, tm, n_groups, epg):
    B, S, D = x.shape
    n_sl = D // LANE
    lstrict = jnp.asarray(np.tril(np.ones((tm, tm), np.float32), -1), BF16)
    const = lambda b, i: (0, 0)
    return pl.pallas_call(
        functools.partial(_outproj_body, n_groups=n_groups, epg=epg),
        grid=(B, S // tm),
        in_specs=[
            pl.BlockSpec((1, mixp.shape[1], tm, LANE), lambda b, i: (b, 0, i, 0)),
            pl.BlockSpec((1, mixh.shape[1], tm, LANE), lambda b, i: (b, 0, i, 0)),
            pl.BlockSpec((1, tm, D), lambda b, i: (b, i, 0)),
            pl.BlockSpec(w_out.shape, const),
            pl.BlockSpec((1, D), const),
            pl.BlockSpec((D, LANE), const),
            pl.BlockSpec((D, LANE), const),
            pl.BlockSpec((1, LANE), const),
            pl.BlockSpec((tm, tm), const),
        ],
        out_specs=[
            pl.BlockSpec((1, tm, D), lambda b, i: (b, i, 0)),
            pl.BlockSpec((tm * n_sl, LANE), lambda b, i: (b * (S // tm) + i, 0)),
            pl.BlockSpec((1, tm, LANE), lambda b, i: (b, i, 0)),
            pl.BlockSpec((1, LANE), const),
        ],
        out_shape=[
            jax.ShapeDtypeStruct((B, S, D), F32),
            jax.ShapeDtypeStruct((B * S * n_sl, LANE), F32),
            jax.ShapeDtypeStruct((B, S, LANE), F32),
            jax.ShapeDtypeStruct((1, LANE), F32),
        ],
        compiler_params=_params(2),
        name="outproj",
    )(mixp, mixh, x, w_out, gain, wr_hi, wr_lo, rbias, lstrict)


def _gather_rows(idx_ref, n, src_ref, dst_ref, sem, rows_per):
    def issue(r, carry):
        s = pl.multiple_of(idx_ref[r] * rows_per, rows_per)
        d = pl.multiple_of(r * rows_per, rows_per)
        pltpu.make_async_copy(src_ref.at[pl.ds(s, rows_per)], dst_ref.at[pl.ds(d, rows_per)], sem).start()
        return carry

    lax.fori_loop(0, n, issue, 0)
    total = n * rows_per
    pltpu.make_async_copy(src_ref.at[pl.ds(0, total)], dst_ref.at[pl.ds(0, total)], sem).wait()


def _slab_rows(ref, n, n_sl):
    return jnp.concatenate([ref[pl.ds(j, n, stride=n_sl), :] for j in range(n_sl)], axis=-1)


def _experts_body(be_ref, nused_ref, tok_ref, xn_ref, w1_ref, w3_ref, w2_ref, y_ref, xbuf, sem, *, n_sl):
    i = pl.program_id(0)
    blk = EXPERT_ROWS

    @pl.when(i < nused_ref[0])
    def _():
        _gather_rows(tok_ref, blk, xn_ref, xbuf, sem, n_sl)
        xb = _slab_rows(xbuf, blk, n_sl).astype(BF16)
        a = jnp.dot(xb, w1_ref[0], preferred_element_type=F32)
        h = (a * jax.nn.sigmoid(a)) * jnp.dot(xb, w3_ref[0], preferred_element_type=F32)
        y = jnp.dot(h.astype(BF16), w2_ref[0], preferred_element_type=F32)
        for j in range(n_sl):
            y_ref[pl.ds(j, blk, stride=n_sl), :] = y[:, j * LANE:(j + 1) * LANE]

    @pl.when(i >= nused_ref[0])
    def _():
        y_ref[...] = jnp.zeros_like(y_ref)


def _experts(block_expert, n_used, slot_tok, xn_rows, w1, w3, w2, n_sl):
    n_blocks = block_expert.shape[0]
    blk = EXPERT_ROWS
    _, D, DE = w1.shape
    grid_spec = pltpu.PrefetchScalarGridSpec(
        num_scalar_prefetch=2,
        grid=(n_blocks,),
        in_specs=[
            pl.BlockSpec((blk,), lambda i, be, nu: (i,), memory_space=pltpu.SMEM),
            pl.BlockSpec(memory_space=pl.ANY),
            pl.BlockSpec((1, D, DE), lambda i, be, nu: (be[i], 0, 0)),
            pl.BlockSpec((1, D, DE), lambda i, be, nu: (be[i], 0, 0)),
            pl.BlockSpec((1, DE, D), lambda i, be, nu: (be[i], 0, 0)),
        ],
        out_specs=pl.BlockSpec((blk * n_sl, LANE), lambda i, be, nu: (i, 0)),
        scratch_shapes=[pltpu.VMEM((blk * n_sl, LANE), F32), pltpu.SemaphoreType.DMA(())],
    )
    return pl.pallas_call(
        functools.partial(_experts_body, n_sl=n_sl),
        grid_spec=grid_spec,
        out_shape=jax.ShapeDtypeStruct((n_blocks * blk * n_sl, LANE), F32),
        compiler_params=_params(1),
        name="experts",
    )(block_expert, n_used, slot_tok, xn_rows, w1, w3, w2)


def _final_body(d1_ref, d2_ref, x2_ref, info_ref, gain_ref, y_ref, o_ref, ybuf1, ybuf2, sem, *, n_sl):
    tg = x2_ref.shape[0]
    _gather_rows(d1_ref, tg, y_ref, ybuf1, sem.at[0], n_sl)
    _gather_rows(d2_ref, tg, y_ref, ybuf2, sem.at[1], n_sl)
    info = info_ref[...]
    g1 = info[:, INFO_GATE:INFO_GATE + 1]
    g2 = info[:, INFO_GATE + 1:INFO_GATE + 2]
    moe = _slab_rows(ybuf1, tg, n_sl) * g1 + _slab_rows(ybuf2, tg, n_sl) * g2
    o_ref[...] = _rms(x2_ref[...] + moe, gain_ref[...])


def _final(dest1, dest2, x2, info, gain, y_rows, tg, n_sl):
    T, D = x2.shape
    return pl.pallas_call(
        functools.partial(_final_body, n_sl=n_sl),
        grid=(T // tg,),
        in_specs=[
            pl.BlockSpec((tg,), lambda i: (i,), memory_space=pltpu.SMEM),
            pl.BlockSpec((tg,), lambda i: (i,), memory_space=pltpu.SMEM),
            pl.BlockSpec((tg, D), lambda i: (i, 0)),
            pl.BlockSpec((tg, LANE), lambda i: (i, 0)),
            pl.BlockSpec((1, D), lambda i: (0, 0)),
            pl.BlockSpec(memory_space=pl.ANY),
        ],
        out_specs=pl.BlockSpec((tg, D), lambda i: (i, 0)),
        out_shape=jax.ShapeDtypeStruct((T, D), F32),
        scratch_shapes=[pltpu.VMEM((tg * n_sl, LANE), F32), pltpu.VMEM((tg * n_sl, LANE), F32),
                        pltpu.SemaphoreType.DMA((2,))],
        compiler_params=_params(1),
        name="final",
    )(dest1, dest2, x2, info, gain, y_rows)


def _tile(n, pref):
    return pref if n % pref == 0 else n


def _layer(x, p):
    B, S, D = x.shape
    T = B * S
    n_sl = D // LANE
    G = p["pool_w"].shape[0]
    H = p["lb_f"].shape[0]
    E = p["w1"].shape[0]
    n_groups = p["n_groups"]
    tm = _tile(S, 512)

    u = _inproj(x, p["norm_mix"], p["w_in"], tm)
    mixp = _pool(u, p["pool_w"], p["pool_scale"])
    mixh = _hgrn(u, p["lb_f"], p["lb_b"], p["hg_gain"], G)
    x2, xn_rows, info, counts = _outproj(mixp, mixh, x, p["w_out"], p["norm_ffn"], p["wr_hi"], p["wr_lo"],
                                         p["rbias"], tm, n_groups, E // n_groups)

    blk = EXPERT_ROWS
    A = T * TOP_K
    n_blocks = -(-A // blk) + E
    cnt = counts[0, n_groups:n_groups + E].astype(jnp.int32)
    padded = (cnt + blk - 1) // blk * blk
    pend = jnp.cumsum(padded)
    pstart = pend - padded
    info2 = info.reshape(T, LANE)
    eid = info2[:, INFO_EXPERT:INFO_EXPERT + TOP_K].astype(jnp.int32)
    rank = info2[:, INFO_RANK:INFO_RANK + TOP_K].astype(jnp.int32)
    dest = pstart[eid] + rank
    tok = jnp.broadcast_to(jnp.arange(T, dtype=jnp.int32)[:, None], (T, TOP_K))
    slot_tok = jnp.zeros((n_blocks * blk,), jnp.int32).at[dest.reshape(-1)].set(tok.reshape(-1))
    block_expert = jnp.minimum(
        jnp.searchsorted(pend, jnp.arange(n_blocks, dtype=jnp.int32) * blk, side="right"), E - 1).astype(jnp.int32)
    n_used = (pend[-1:] // blk).astype(jnp.int32)

    y_rows = _experts(block_expert, n_used, slot_tok, xn_rows, p["w1"], p["w3"], p["w2"], n_sl)
    out = _final(dest[:, 0], dest[:, 1], x2.reshape(T, D), info2, p["norm_final"], y_rows, _tile(T, 256), n_sl)
    return out.reshape(B, S, D)


def kernel(x_prompt, x_sample, w_in, w_out, pool_w, pool_scale, hg_lb_fwd, hg_lb_bwd, hg_norm_gain, norm_mix, norm_ffn, router_group_w, router_group_b, router_expert_w, router_expert_b, expert_w1, expert_w3, expert_w2, norm_final):
    depth = w_in.shape[0]
    assert depth == 1, "the final norm is fused into the last layer's combine kernel; one layer supported"
    D = w_in.shape[1]
    hg_width = hg_lb_fwd.shape[1]
    dv = hg_norm_gain.shape[1]
    pg = pool_w.shape[2]
    assert dv == LANE and pg == LANE and D % LANE == 0
    H = hg_width // dv
    n_groups = router_group_w.shape[-1]
    E = router_expert_w.shape[-1]
    assert n_groups + E <= LANE

    lb_f = jnp.cumsum(jax.nn.softmax(hg_lb_fwd.astype(F32), axis=0), axis=0)
    lb_b = jnp.cumsum(jax.nn.softmax(hg_lb_bwd.astype(F32), axis=0), axis=0)
    l = 0
    wr = jnp.concatenate([router_group_w[l], router_expert_w[l]], axis=1).astype(F32)
    wr = jnp.pad(wr, ((0, 0), (0, LANE - wr.shape[1])))
    wr_hi = wr.astype(BF16)
    rbias = jnp.concatenate([router_group_b[l], router_expert_b[l]]).astype(F32)
    p = dict(
        n_groups=n_groups,
        w_in=w_in[l].astype(BF16), w_out=w_out[l].astype(BF16),
        pool_w=pool_w[l], pool_scale=pool_scale[l],
        lb_f=lb_f[l].reshape(H, 1, dv), lb_b=lb_b[l].reshape(H, 1, dv), hg_gain=hg_norm_gain[l].reshape(1, dv).astype(F32),
        norm_mix=norm_mix[l].reshape(1, D).astype(F32), norm_ffn=norm_ffn[l].reshape(1, D).astype(F32),
        norm_final=norm_final.reshape(1, D).astype(F32),
        wr_hi=wr_hi, wr_lo=(wr - wr_hi.astype(F32)).astype(BF16),
        rbias=jnp.pad(rbias, (0, LANE - rbias.shape[0])).reshape(1, LANE),
        w1=expert_w1[l].astype(BF16), w3=expert_w3[l].astype(BF16), w2=expert_w2[l].astype(BF16),
    )
    return (_layer(x_prompt, p), _layer(x_sample, p))
```

```python
import functools

import jax
import jax.numpy as jnp
import numpy as np
from jax import lax
from jax.experimental import pallas as pl
from jax.experimental.pallas import tpu as pltpu

F32 = jnp.float32
BF16 = jnp.bfloat16

EPS = 1e-6
POOL_WINDOWS = (2, 4, 8, 16)
TOP_K = 2

LANE = 128
SUBLANE = 8
VMEM_LIMIT = 56 * 1024 * 1024

HG_CHUNK = 32
HG_GROUP = 128
HG_UNROLL = 2
POOL_ROWS = 128
EXPERT_ROWS = 256


def _params(n_axes):
    return pltpu.CompilerParams(dimension_semantics=("arbitrary",) * n_axes, vmem_limit_bytes=VMEM_LIMIT)


def _rms(x, gain):
    return x * lax.rsqrt(jnp.mean(x * x, axis=-1, keepdims=True) + EPS) * gain


def _inproj_body(x_ref, gain_ref, w_ref, u_ref, *, nc):
    n = _rms(x_ref[0], gain_ref[...]).astype(BF16)
    per = nc // LANE
    for c in range(w_ref.shape[1] // nc):
        r = jnp.dot(n, w_ref[:, c * nc:(c + 1) * nc], preferred_element_type=F32)
        for j in range(per):
            u_ref[0, c * per + j] = r[:, j * LANE:(j + 1) * LANE].astype(BF16)


def _inproj(x, gain, w, tm):
    B, S, D = x.shape
    cols = w.shape[1]
    return pl.pallas_call(
        functools.partial(_inproj_body, nc=4 * LANE),
        grid=(B, S // tm),
        in_specs=[
            pl.BlockSpec((1, tm, D), lambda b, i: (b, i, 0)),
            pl.BlockSpec((1, D), lambda b, i: (0, 0)),
            pl.BlockSpec((D, cols), lambda b, i: (0, 0)),
        ],
        out_specs=pl.BlockSpec((1, cols // LANE, tm, LANE), lambda b, i: (b, 0, i, 0)),
        out_shape=jax.ShapeDtypeStruct((B, cols // LANE, S, LANE), BF16),
        compiler_params=_params(2),
        name="inproj",
    )(x, gain, w)


def _band_matrices(rows):
    t = np.arange(rows)[:, None]
    s = np.arange(rows)[None, :]
    out = np.zeros((len(POOL_WINDOWS), 3, rows, rows), np.float32)
    for gi, w in enumerate(POOL_WINDOWS):
        for k, shift in enumerate((-rows, 0, rows)):
            pos = s + shift
            out[gi, k] = (pos >= t - w // 2) & (pos < t + w // 2)
    return out


def _pool_body(half_ref, u_ref, band_ref, pw_ref, sc_ref, o_ref, *, seq, rows):
    nt = seq // rows
    h = half_ref[pl.program_id(1)]

    def tile(i, carry):
        r0 = pl.multiple_of(i * rows, rows)
        rp = pl.multiple_of(jnp.maximum(i - 1, 0) * rows, rows)
        rn = pl.multiple_of(jnp.minimum(i + 1, nt - 1) * rows, rows)
        xc = u_ref[0, 0, pl.ds(r0, rows), :]
        s = jnp.dot(band_ref[0, 1], xc, preferred_element_type=F32)
        sp = jnp.dot(band_ref[0, 0], u_ref[0, 0, pl.ds(rp, rows), :], preferred_element_type=F32)
        sn = jnp.dot(band_ref[0, 2], u_ref[0, 0, pl.ds(rn, rows), :], preferred_element_type=F32)
        s = s + jnp.where(i > 0, sp, 0.0) + jnp.where(i < nt - 1, sn, 0.0)
        t = r0 + lax.broadcasted_iota(jnp.int32, (rows, LANE), 0)
        cnt = (jnp.minimum(t + h, seq) - jnp.maximum(t - h, 0)).astype(F32)
        pooled = s / cnt - xc.astype(F32)
        y = jnp.dot(pooled.astype(BF16), pw_ref[0], preferred_element_type=F32) * sc_ref[0]
        o_ref[0, 0, pl.ds(r0, rows), :] = y.astype(BF16)
        return carry

    lax.fori_loop(0, nt, tile, 0)


def _pool(u, pool_w, pool_scale):
    B, _, S, _ = u.shape
    G = pool_w.shape[0]
    rows = min(POOL_ROWS, S)
    band = jnp.asarray(_band_matrices(rows), BF16)
    halves = jnp.asarray([w // 2 for w in POOL_WINDOWS], jnp.int32)
    grid_spec = pltpu.PrefetchScalarGridSpec(
        num_scalar_prefetch=1,
        grid=(B, G),
        in_specs=[
            pl.BlockSpec((1, 1, S, LANE), lambda b, g, h: (b, g, 0, 0)),
            pl.BlockSpec((1, 3, rows, rows), lambda b, g, h: (g, 0, 0, 0)),
            pl.BlockSpec((1, LANE, LANE), lambda b, g, h: (g, 0, 0)),
            pl.BlockSpec((1, 1, LANE), lambda b, g, h: (g, 0, 0)),
        ],
        out_specs=pl.BlockSpec((1, 1, S, LANE), lambda b, g, h: (b, g, 0, 0)),
    )
    return pl.pallas_call(
        functools.partial(_pool_body, seq=S, rows=rows),
        grid_spec=grid_spec,
        out_shape=jax.ShapeDtypeStruct((B, G, S, LANE), BF16),
        compiler_params=_params(2),
        name="pool",
    )(halves, u, band, pool_w.astype(BF16), pool_scale.reshape(G, 1, LANE).astype(F32))


def _hg_gates(q_ref, f_ref, v_ref, r0, lb):
    rows = pl.ds(r0, HG_GROUP)
    q = q_ref[0, 0, rows, :].astype(F32)
    q = q * jax.nn.sigmoid(q)
    f = lb + (1.0 - lb) * jax.nn.sigmoid(f_ref[0, 0, rows, :].astype(F32))
    g = jnp.log(f)
    g_hi = g.astype(BF16)
    g_lo = (g - g_hi.astype(F32)).astype(BF16)
    return q, 1.0 - f, v_ref[0, 0, rows, :], g_hi, g_lo


def _chunk_rows(b, row):
    C = HG_CHUNK
    return jnp.concatenate(
        [jnp.broadcast_to(b[c * C + row:c * C + row + 1], (C, LANE)) for c in range(HG_GROUP // C)], axis=0)


def _hg_decays(q, k, b, *, reverse):
    C = HG_CHUNK
    b_end = _chunk_rows(b, 0 if reverse else C - 1)
    b_mid = _chunk_rows(b, C // 2 if reverse else C // 2 - 1)
    q_dec = (q * jnp.exp(b)).astype(BF16)
    q_in = (q * jnp.exp(b - b_mid)).astype(BF16)
    k_in = (k * jnp.exp(b_mid - b)).astype(BF16)
    k_st = (k * jnp.exp(b_end - b)).astype(BF16)
    dec = [jnp.exp(b_end[c * C:c * C + 1]) for c in range(HG_GROUP // C)]
    return q_dec, q_in, k_in, k_st, dec


def _hgrn_body(q_ref, ff_ref, fb_ref, v_ref, g_ref, lbf_ref, lbb_ref, gain_ref, tri_ref,
               o_ref, of_scr, ob_scr, *, seq, rows, unroll):
    C = HG_CHUNK
    R = HG_GROUP
    n_c = R // C
    n_groups = seq // R
    lb = (lbf_ref[0], lbb_ref[0])
    f_refs = (ff_ref, fb_ref)
    scr = (of_scr, ob_scr)
    ri = lax.broadcasted_iota(jnp.int32, (R, R), 0)
    ci = lax.broadcasted_iota(jnp.int32, (R, R), 1)
    same = (ri // C) == (ci // C)
    masks = (same & (ci <= ri), same & (ci >= ri))
    contract_last = (((1,), (1,)), ((), ()))
    contract_first = (((0,), (0,)), ((), ()))

    def step(j, carry):
        streams = []
        for d in range(2):
            for i in range(unroll):
                gi = j * unroll + i
                streams.append((d, pl.multiple_of((gi if d == 0 else n_groups - 1 - gi) * R, R)))
        gates = [_hg_gates(q_ref, f_refs[d], v_ref, r0, lb[d]) for d, r0 in streams]
        cums = [jnp.dot(tri_ref[d], g_hi, preferred_element_type=F32)
                + jnp.dot(tri_ref[d], g_lo, preferred_element_type=F32)
                for (d, _), (_, _, _, g_hi, g_lo) in zip(streams, gates)]
        ops = [_hg_decays(q, k, b, reverse=(d == 1)) for (d, _), (q, k, _, _, _), b in zip(streams, gates, cums)]
        scores = [lax.dot_general(q_in, k_in, contract_last, preferred_element_type=F32)
                  for _, q_in, k_in, _, _ in ops]
        kvs = [[lax.dot_general(v[c * C:(c + 1) * C], k_st[c * C:(c + 1) * C], contract_first,
                                preferred_element_type=F32) for c in range(n_c)]
               for (_, _, v, _, _), (_, _, _, k_st, _) in zip(gates, ops)]
        intra = [jnp.dot(jnp.where(masks[d], s, 0.0).astype(BF16), v, preferred_element_type=F32)
                 for (d, _), s, (_, _, v, _, _) in zip(streams, scores, gates)]
        states = list(carry)
        for n, (d, r0) in enumerate(streams):
            q_dec, dec = ops[n][0], ops[n][4]
            inter = [None] * n_c
            for c in (range(n_c) if d == 0 else reversed(range(n_c))):
                inter[c] = lax.dot_general(q_dec[c * C:(c + 1) * C], states[d].astype(BF16), contract_last,
                                           preferred_element_type=F32)
                states[d] = states[d] * dec[c] + kvs[n][c]
            scr[d][pl.ds(r0, R), :] = intra[n] + jnp.concatenate(inter, axis=0)
        return tuple(states)

    zero = jnp.zeros((LANE, LANE), F32)
    lax.fori_loop(0, n_groups // unroll, step, (zero, zero))

    def finish(i, carry):
        r = pl.ds(pl.multiple_of(i * rows, rows), rows)
        o = of_scr[r, :] + ob_scr[r, :]
        o = _rms(o, gain_ref[...])
        gate = g_ref[0, 0, r, :].astype(F32)
        o_ref[0, 0, r, :] = (o * (gate * jax.nn.sigmoid(gate))).astype(BF16)
        return carry

    lax.fori_loop(0, seq // rows, finish, 0)


def _hgrn(u, lb_f, lb_b, gain, n_pool):
    B, _, S, _ = u.shape
    H = lb_f.shape[0]
    C = HG_CHUNK
    R = HG_GROUP
    assert S % R == 0
    low = np.kron(np.eye(R // C), np.tril(np.ones((C, C)))).astype(np.float32)
    tri = jnp.asarray(np.stack([low, low.T]), BF16)
    rows = min(256, S)
    unroll = HG_UNROLL if (S // R) % HG_UNROLL == 0 else 1

    def slab(k):
        return pl.BlockSpec((1, 1, S, LANE), lambda b, h, k=k: (b, n_pool + k * H + h, 0, 0))

    head_vec = pl.BlockSpec((1, 1, LANE), lambda b, h: (h, 0, 0))
    return pl.pallas_call(
        functools.partial(_hgrn_body, seq=S, rows=rows, unroll=unroll),
        grid=(B, H),
        in_specs=[slab(0), slab(1), slab(2), slab(3), slab(4), head_vec, head_vec,
                  pl.BlockSpec((1, LANE), lambda b, h: (0, 0)),
                  pl.BlockSpec((2, R, R), lambda b, h: (0, 0, 0))],
        out_specs=pl.BlockSpec((1, 1, S, LANE), lambda b, h: (b, h, 0, 0)),
        out_shape=jax.ShapeDtypeStruct((B, H, S, LANE), BF16),
        scratch_shapes=[pltpu.VMEM((S, LANE), F32), pltpu.VMEM((S, LANE), F32)],
        compiler_params=_params(2),
        name="hgrn",
    )(u, u, u, u, u, lb_f, lb_b, gain, tri)


INFO_GATE = 0
INFO_EXPERT = 2
INFO_RANK = 4


def _outproj_body(mp_ref, mh_ref, x_ref, wo_ref, gain_ref, wrh_ref, wrl_ref, rb_ref, ls_ref,
                  x2_ref, xn_ref, info_ref, cnt_ref, *, n_groups, epg):
    tm = x_ref.shape[1]

    @pl.when((pl.program_id(0) == 0) & (pl.program_id(1) == 0))
    def _():
        cnt_ref[...] = jnp.zeros_like(cnt_ref)

    mix = jnp.concatenate([mp_ref[0, j] for j in range(mp_ref.shape[1])]
                          + [mh_ref[0, j] for j in range(mh_ref.shape[1])], axis=-1)
    x2 = x_ref[0] + jnp.dot(mix, wo_ref[...], preferred_element_type=F32)
    x2_ref[0] = x2
    xn = _rms(x2, gain_ref[...])
    for j in range(xn.shape[1] // LANE):
        xn_ref[pl.ds(j, tm, stride=xn.shape[1] // LANE), :] = xn[:, j * LANE:(j + 1) * LANE]

    xh = xn.astype(BF16)
    xl = (xn - xh.astype(F32)).astype(BF16)
    logits = (jnp.dot(xh, wrh_ref[...], preferred_element_type=F32)
              + jnp.dot(xl, wrh_ref[...], preferred_element_type=F32)
              + jnp.dot(xh, wrl_ref[...], preferred_element_type=F32)) + rb_ref[...]

    lane = lax.broadcasted_iota(jnp.int32, (tm, LANE), 1)
    neg = jnp.float32(-jnp.inf)
    big = jnp.int32(LANE)

    def top(vals):
        m = jnp.max(vals, axis=-1, keepdims=True)
        return m, jnp.min(jnp.where(vals == m, lane, big), axis=-1, keepdims=True)

    glog = jnp.where(lane < n_groups, logits, neg)
    gmax, grp = top(glog)
    grp_prob = 1.0 / jnp.sum(jnp.exp(glog - gmax), axis=-1, keepdims=True)
    e_lo = n_groups + grp * epg
    elog = jnp.where((lane >= e_lo) & (lane < e_lo + epg), logits, neg)
    v1, i1 = top(elog)
    v2, i2 = top(jnp.where(lane == i1, neg, elog))
    e21 = jnp.exp(v2 - v1)
    gate1 = grp_prob / (1.0 + e21)
    gate2 = grp_prob * e21 / (1.0 + e21)

    hot1 = lane == i1
    hot2 = lane == i2
    onehot = jnp.where(hot1 | hot2, 1.0, 0.0)
    before = jnp.dot(ls_ref[...], onehot.astype(BF16), preferred_element_type=F32) + cnt_ref[...]
    rank1 = jnp.sum(jnp.where(hot1, before, 0.0), axis=-1, keepdims=True)
    rank2 = jnp.sum(jnp.where(hot2, before, 0.0), axis=-1, keepdims=True)
    cnt_ref[...] = cnt_ref[...] + jnp.sum(onehot, axis=0, keepdims=True)

    info = jnp.zeros((tm, LANE), F32)
    for k, col in ((INFO_GATE, gate1), (INFO_GATE + 1, gate2),
                   (INFO_EXPERT, (i1 - n_groups).astype(F32)), (INFO_EXPERT + 1, (i2 - n_groups).astype(F32)),
                   (INFO_RANK, rank1), (INFO_RANK + 1, rank2)):
        info = jnp.where(lane == k, col, info)
    info_ref[0] = info


def _outproj(mixp, mixh, x, w_out, gain, wr_hi, wr_lo, rbias, tm, n_groups, epg):
    B, S, D = x.shape
    n_sl = D // LANE
    lstrict = jnp.asarray(np.tril(np.ones((tm, tm), np.float32), -1), BF16)
    const = lambda b, i: (0, 0)
    return pl.pallas_call(
        functools.partial(_outproj_body, n_groups=n_groups, epg=epg),
        grid=(B, S // tm),
        in_specs=[
            pl.BlockSpec((1, mixp.shape[1], tm, LANE), lambda b, i: (b, 0, i, 0)),
            pl.BlockSpec((1, mixh.shape[1], tm, LANE), lambda b, i: (b, 0, i, 0)),
            pl.BlockSpec((1, tm, D), lambda b, i: (b, i, 0)),
            pl.BlockSpec(w_out.shape, const),
            pl.BlockSpec((1, D), const),
            pl.BlockSpec((D, LANE), const),
            pl.BlockSpec((D, LANE), const),
            pl.BlockSpec((1, LANE), const),
            pl.BlockSpec((tm, tm), const),
        ],
        out_specs=[
            pl.BlockSpec((1, tm, D), lambda b, i: (b, i, 0)),
            pl.BlockSpec((tm * n_sl, LANE), lambda b, i: (b * (S // tm) + i, 0)),
            pl.BlockSpec((1, tm, LANE), lambda b, i: (b, i, 0)),
            pl.BlockSpec((1, LANE), const),
        ],
        out_shape=[
            jax.ShapeDtypeStruct((B, S, D), F32),
            jax.ShapeDtypeStruct((B * S * n_sl, LANE), F32),
            jax.ShapeDtypeStruct((B, S, LANE), F32),
            jax.ShapeDtypeStruct((1, LANE), F32),
        ],
        compiler_params=_params(2),
        name="outproj",
    )(mixp, mixh, x, w_out, gain, wr_hi, wr_lo, rbias, lstrict)


def _start_rows(idx_ref, first, n, src_ref, dst_ref, sem, rows_per):
    for r in range(n):
        s = pl.multiple_of(idx_ref[first + r] * rows_per, rows_per)
        pltpu.make_async_copy(src_ref.at[pl.ds(s, rows_per)], dst_ref.at[pl.ds(r * rows_per, rows_per)], sem).start()


def _wait_rows(n, src_ref, dst_ref, sem, rows_per):
    total = n * rows_per
    pltpu.make_async_copy(src_ref.at[pl.ds(0, total)], dst_ref.at[pl.ds(0, total)], sem).wait()


def _slab_rows(ref, n, n_sl):
    return jnp.concatenate([ref[pl.ds(j, n, stride=n_sl), :] for j in range(n_sl)], axis=-1)


def _experts_body(bexp_ref, tok_ref, nxt_ref, xn_ref, w1a_ref, w3a_ref, w2a_ref, w1b_ref, w3b_ref, w2b_ref,
                  y_ref, buf_a, buf_b, sem, *, n_sl):
    i = pl.program_id(0)
    blk = EXPERT_ROWS

    def mlp(buf, w1_ref, w3_ref, w2_ref, half):
        xb = _slab_rows(buf, blk, n_sl).astype(BF16)
        a = jnp.dot(xb, w1_ref[0], preferred_element_type=F32)
        h = (a * jax.nn.sigmoid(a)) * jnp.dot(xb, w3_ref[0], preferred_element_type=F32)
        y = jnp.dot(h.astype(BF16), w2_ref[0], preferred_element_type=F32)
        for j in range(n_sl):
            y_ref[pl.ds(half * blk * n_sl + j, blk, stride=n_sl), :] = y[:, j * LANE:(j + 1) * LANE]

    @pl.when(i == 0)
    def _():
        _start_rows(tok_ref, 0, blk, xn_ref, buf_a, sem.at[0], n_sl)

    _start_rows(tok_ref, blk, blk, xn_ref, buf_b, sem.at[1], n_sl)
    _wait_rows(blk, xn_ref, buf_a, sem.at[0], n_sl)
    mlp(buf_a, w1a_ref, w3a_ref, w2a_ref, 0)
    _start_rows(nxt_ref, 0, blk, xn_ref, buf_a, sem.at[0], n_sl)
    _wait_rows(blk, xn_ref, buf_b, sem.at[1], n_sl)
    mlp(buf_b, w1b_ref, w3b_ref, w2b_ref, 1)

    @pl.when(i == pl.num_programs(0) - 1)
    def _():
        _wait_rows(blk, xn_ref, buf_a, sem.at[0], n_sl)


def _experts(block_expert, slot_tok, xn_rows, w1, w3, w2, n_sl):
    n_blocks = block_expert.shape[0]
    assert n_blocks % 2 == 0
    steps = n_blocks // 2
    blk = EXPERT_ROWS
    _, D, DE = w1.shape

    def wspec(shape, half):
        return pl.BlockSpec(shape, lambda i, be, half=half: (be[2 * i + half], 0, 0))

    grid_spec = pltpu.PrefetchScalarGridSpec(
        num_scalar_prefetch=1,
        grid=(steps,),
        in_specs=[
            pl.BlockSpec((2 * blk,), lambda i, be: (i,), memory_space=pltpu.SMEM),
            pl.BlockSpec((blk,), lambda i, be: (jnp.minimum(2 * i + 2, n_blocks - 2),), memory_space=pltpu.SMEM),
            pl.BlockSpec(memory_space=pl.ANY),
            wspec((1, D, DE), 0), wspec((1, D, DE), 0), wspec((1, DE, D), 0),
            wspec((1, D, DE), 1), wspec((1, D, DE), 1), wspec((1, DE, D), 1),
        ],
        out_specs=pl.BlockSpec((2 * blk * n_sl, LANE), lambda i, be: (i, 0)),
        scratch_shapes=[pltpu.VMEM((blk * n_sl, LANE), F32), pltpu.VMEM((blk * n_sl, LANE), F32),
                        pltpu.SemaphoreType.DMA((2,))],
    )
    return pl.pallas_call(
        functools.partial(_experts_body, n_sl=n_sl),
        grid_spec=grid_spec,
        out_shape=jax.ShapeDtypeStruct((n_blocks * blk * n_sl, LANE), F32),
        compiler_params=_params(1),
        name="experts",
    )(block_expert, slot_tok, slot_tok, xn_rows, w1, w3, w2, w1, w3, w2)


def _final_body(d1_ref, d2_ref, n1_ref, n2_ref, x2_ref, info_ref, gain_ref, y_ref, o_ref,
                a1, a2, b1, b2, sem, *, n_sl, tg):
    i = pl.program_id(0)

    def combine(buf1, buf2, half):
        rows = pl.ds(half * tg, tg)
        info = info_ref[rows, :]
        moe = (_slab_rows(buf1, tg, n_sl) * info[:, INFO_GATE:INFO_GATE + 1]
               + _slab_rows(buf2, tg, n_sl) * info[:, INFO_GATE + 1:INFO_GATE + 2])
        o_ref[rows, :] = _rms(x2_ref[rows, :] + moe, gain_ref[...])

    def start(i1_ref, i2_ref, first, buf1, buf2, s):
        _start_rows(i1_ref, first, tg, y_ref, buf1, sem.at[s], n_sl)
        _start_rows(i2_ref, first, tg, y_ref, buf2, sem.at[s], n_sl)

    def wait(buf1, buf2, s):
        _wait_rows(tg, y_ref, buf1, sem.at[s], n_sl)
        _wait_rows(tg, y_ref, buf2, sem.at[s], n_sl)

    @pl.when(i == 0)
    def _():
        start(d1_ref, d2_ref, 0, a1, a2, 0)

    start(d1_ref, d2_ref, tg, b1, b2, 1)
    wait(a1, a2, 0)
    combine(a1, a2, 0)
    start(n1_ref, n2_ref, 0, a1, a2, 0)
    wait(b1, b2, 1)
    combine(b1, b2, 1)

    @pl.when(i == pl.num_programs(0) - 1)
    def _():
        wait(a1, a2, 0)


def _final(dest1, dest2, x2, info, gain, y_rows, tg, n_sl):
    T, D = x2.shape
    assert T % (2 * tg) == 0
    steps = T // (2 * tg)
    cur = lambda i: (i,)
    nxt = lambda i: (jnp.minimum(2 * i + 2, 2 * steps - 1),)
    buf = pltpu.VMEM((tg * n_sl, LANE), F32)
    return pl.pallas_call(
        functools.partial(_final_body, n_sl=n_sl, tg=tg),
        grid=(steps,),
        in_specs=[
            pl.BlockSpec((2 * tg,), cur, memory_space=pltpu.SMEM),
            pl.BlockSpec((2 * tg,), cur, memory_space=pltpu.SMEM),
            pl.BlockSpec((tg,), nxt, memory_space=pltpu.SMEM),
            pl.BlockSpec((tg,), nxt, memory_space=pltpu.SMEM),
            pl.BlockSpec((2 * tg, D), lambda i: (i, 0)),
            pl.BlockSpec((2 * tg, LANE), lambda i: (i, 0)),
            pl.BlockSpec((1, D), lambda i: (0, 0)),
            pl.BlockSpec(memory_space=pl.ANY),
        ],
        out_specs=pl.BlockSpec((2 * tg, D), lambda i: (i, 0)),
        out_shape=jax.ShapeDtypeStruct((T, D), F32),
        scratch_shapes=[buf, buf, buf, buf, pltpu.SemaphoreType.DMA((2,))],
        compiler_params=_params(1),
        name="final",
    )(dest1, dest2, dest1, dest2, x2, info, gain, y_rows)


def _tile(n, pref):
    return pref if n % pref == 0 else n


def _layer(x, p):
    B, S, D = x.shape
    T = B * S
    n_sl = D // LANE
    G = p["pool_w"].shape[0]
    H = p["lb_f"].shape[0]
    E = p["w1"].shape[0]
    n_groups = p["n_groups"]
    tm = _tile(S, 512)

    u = _inproj(x, p["norm_mix"], p["w_in"], tm)
    mixp = _pool(u, p["pool_w"], p["pool_scale"])
    mixh = _hgrn(u, p["lb_f"], p["lb_b"], p["hg_gain"], G)
    x2, xn_rows, info, counts = _outproj(mixp, mixh, x, p["w_out"], p["norm_ffn"], p["wr_hi"], p["wr_lo"],
                                         p["rbias"], tm, n_groups, E // n_groups)

    blk = EXPERT_ROWS
    A = T * TOP_K
    n_blocks = -(-A // blk) + E
    n_blocks += n_blocks % 2
    cnt = counts[0, n_groups:n_groups + E].astype(jnp.int32)
    padded = (cnt + blk - 1) // blk * blk
    pend = jnp.cumsum(padded)
    pstart = pend - padded
    info2 = info.reshape(T, LANE)
    eid = info2[:, INFO_EXPERT:INFO_EXPERT + TOP_K].astype(jnp.int32)
    rank = info2[:, INFO_RANK:INFO_RANK + TOP_K].astype(jnp.int32)
    dest = pstart[eid] + rank
    tok = jnp.broadcast_to(jnp.arange(T, dtype=jnp.int32)[:, None], (T, TOP_K))
    slot_tok = jnp.zeros((n_blocks * blk,), jnp.int32).at[dest.reshape(-1)].set(
        tok.reshape(-1), unique_indices=True, mode="promise_in_bounds")
    first_slot = jnp.arange(n_blocks, dtype=jnp.int32) * blk
    block_expert = jnp.minimum(jnp.sum(pend[None, :] <= first_slot[:, None], axis=1), E - 1).astype(jnp.int32)

    y_rows = _experts(block_expert, slot_tok, xn_rows, p["w1"], p["w3"], p["w2"], n_sl)
    tg = 128 if T % 256 == 0 else T // 2
    out = _final(dest[:, 0], dest[:, 1], x2.reshape(T, D), info2, p["norm_final"], y_rows, tg, n_sl)
    return out.reshape(B, S, D)


def kernel(x_prompt, x_sample, w_in, w_out, pool_w, pool_scale, hg_lb_fwd, hg_lb_bwd, hg_norm_gain, norm_mix, norm_ffn, router_group_w, router_group_b, router_expert_w, router_expert_b, expert_w1, expert_w3, expert_w2, norm_final):
    depth = w_in.shape[0]
    assert depth == 1, "the final norm is fused into the last layer's combine kernel; one layer supported"
    D = w_in.shape[1]
    hg_width = hg_lb_fwd.shape[1]
    dv = hg_norm_gain.shape[1]
    pg = pool_w.shape[2]
    assert dv == LANE and pg == LANE and D % LANE == 0
    H = hg_width // dv
    n_groups = router_group_w.shape[-1]
    E = router_expert_w.shape[-1]
    assert n_groups + E <= LANE

    lb_f = jnp.cumsum(jax.nn.softmax(hg_lb_fwd.astype(F32), axis=0), axis=0)
    lb_b = jnp.cumsum(jax.nn.softmax(hg_lb_bwd.astype(F32), axis=0), axis=0)
    l = 0
    wr = jnp.concatenate([router_group_w[l], router_expert_w[l]], axis=1).astype(F32)
    wr = jnp.pad(wr, ((0, 0), (0, LANE - wr.shape[1])))
    wr_hi = wr.astype(BF16)
    rbias = jnp.concatenate([router_group_b[l], router_expert_b[l]]).astype(F32)
    p = dict(
        n_groups=n_groups,
        w_in=w_in[l].astype(BF16), w_out=w_out[l].astype(BF16),
        pool_w=pool_w[l], pool_scale=pool_scale[l],
        lb_f=lb_f[l].reshape(H, 1, dv), lb_b=lb_b[l].reshape(H, 1, dv), hg_gain=hg_norm_gain[l].reshape(1, dv).astype(F32),
        norm_mix=norm_mix[l].reshape(1, D).astype(F32), norm_ffn=norm_ffn[l].reshape(1, D).astype(F32),
        norm_final=norm_final.reshape(1, D).astype(F32),
        wr_hi=wr_hi, wr_lo=(wr - wr_hi.astype(F32)).astype(BF16),
        rbias=jnp.pad(rbias, (0, LANE - rbias.shape[0])).reshape(1, LANE),
        w1=expert_w1[l].astype(BF16), w3=expert_w3[l].astype(BF16), w2=expert_w2[l].astype(BF16),
    )
    return (_layer(x_prompt, p), _layer(x_sample, p))
```

```python
import functools

import jax
import jax.numpy as jnp
import numpy as np
from jax import lax
from jax.experimental import pallas as pl
from jax.experimental.pallas import tpu as pltpu

F32 = jnp.float32
BF16 = jnp.bfloat16

EPS = 1e-6
POOL_WINDOWS = (2, 4, 8, 16)
TOP_K = 2

LANE = 128
SUBLANE = 8
VMEM_LIMIT = 56 * 1024 * 1024

HG_CHUNK = 32
HG_GROUP = 128
HG_UNROLL = 2
POOL_ROWS = 128
POOL_UNROLL = 4
EXPERT_ROWS = 256


def _params(n_axes):
    return pltpu.CompilerParams(dimension_semantics=("arbitrary",) * n_axes, vmem_limit_bytes=VMEM_LIMIT)


def _rms(x, gain):
    return x * lax.rsqrt(jnp.mean(x * x, axis=-1, keepdims=True) + EPS) * gain


def _inproj_body(x_ref, gain_ref, w_ref, u_ref, *, nc):
    n = _rms(x_ref[0], gain_ref[...]).astype(BF16)
    per = nc // LANE
    for c in range(w_ref.shape[1] // nc):
        r = jnp.dot(n, w_ref[:, c * nc:(c + 1) * nc], preferred_element_type=F32)
        for j in range(per):
            u_ref[0, c * per + j] = r[:, j * LANE:(j + 1) * LANE].astype(BF16)


def _inproj(x, gain, w, tm):
    B, S, D = x.shape
    cols = w.shape[1]
    return pl.pallas_call(
        functools.partial(_inproj_body, nc=4 * LANE),
        grid=(B, S // tm),
        in_specs=[
            pl.BlockSpec((1, tm, D), lambda b, i: (b, i, 0)),
            pl.BlockSpec((1, D), lambda b, i: (0, 0)),
            pl.BlockSpec((D, cols), lambda b, i: (0, 0)),
        ],
        out_specs=pl.BlockSpec((1, cols // LANE, tm, LANE), lambda b, i: (b, 0, i, 0)),
        out_shape=jax.ShapeDtypeStruct((B, cols // LANE, S, LANE), BF16),
        compiler_params=_params(2),
        name="inproj",
    )(x, gain, w)


def _band_matrices(rows):
    t = np.arange(rows)[:, None]
    s = np.arange(rows)[None, :]
    out = np.zeros((len(POOL_WINDOWS), 3, rows, rows), np.float32)
    for gi, w in enumerate(POOL_WINDOWS):
        for k, shift in enumerate((-rows, 0, rows)):
            pos = s + shift
            out[gi, k] = (pos >= t - w // 2) & (pos < t + w // 2)
    return out


def _pool_body(half_ref, u_ref, band_ref, pw_ref, sc_ref, o_ref, *, seq, rows, unroll):
    nt = seq // rows
    h = half_ref[pl.program_id(1)]

    def window_sum(i):
        r0 = pl.multiple_of(i * rows, rows)
        rp = pl.multiple_of(jnp.maximum(i - 1, 0) * rows, rows)
        rn = pl.multiple_of(jnp.minimum(i + 1, nt - 1) * rows, rows)
        xc = u_ref[0, 0, pl.ds(r0, rows), :]
        s = jnp.dot(band_ref[0, 1], xc, preferred_element_type=F32)
        sp = jnp.dot(band_ref[0, 0], u_ref[0, 0, pl.ds(rp, rows), :], preferred_element_type=F32)
        sn = jnp.dot(band_ref[0, 2], u_ref[0, 0, pl.ds(rn, rows), :], preferred_element_type=F32)
        return r0, xc, s + jnp.where(i > 0, sp, 0.0) + jnp.where(i < nt - 1, sn, 0.0)

    def pooled(r0, xc, s):
        t = r0 + lax.broadcasted_iota(jnp.int32, (rows, LANE), 0)
        cnt = (jnp.minimum(t + h, seq) - jnp.maximum(t - h, 0)).astype(F32)
        return (s / cnt - xc.astype(F32)).astype(BF16)

    def tiles(j, carry):
        sums = [window_sum(j * unroll + k) for k in range(unroll)]
        pools = [pooled(*a) for a in sums]
        ys = [jnp.dot(pv, pw_ref[0], preferred_element_type=F32) * sc_ref[0] for pv in pools]
        for (r0, _, _), y in zip(sums, ys):
            o_ref[0, 0, pl.ds(r0, rows), :] = y.astype(BF16)
        return carry

    lax.fori_loop(0, nt // unroll, tiles, 0)


def _pool(u, pool_w, pool_scale):
    B, _, S, _ = u.shape
    G = pool_w.shape[0]
    rows = min(POOL_ROWS, S)
    band = jnp.asarray(_band_matrices(rows), BF16)
    halves = jnp.asarray([w // 2 for w in POOL_WINDOWS], jnp.int32)
    grid_spec = pltpu.PrefetchScalarGridSpec(
        num_scalar_prefetch=1,
        grid=(B, G),
        in_specs=[
            pl.BlockSpec((1, 1, S, LANE), lambda b, g, h: (b, g, 0, 0)),
            pl.BlockSpec((1, 3, rows, rows), lambda b, g, h: (g, 0, 0, 0)),
            pl.BlockSpec((1, LANE, LANE), lambda b, g, h: (g, 0, 0)),
            pl.BlockSpec((1, 1, LANE), lambda b, g, h: (g, 0, 0)),
        ],
        out_specs=pl.BlockSpec((1, 1, S, LANE), lambda b, g, h: (b, g, 0, 0)),
    )
    return pl.pallas_call(
        functools.partial(_pool_body, seq=S, rows=rows, unroll=POOL_UNROLL if (S // rows) % POOL_UNROLL == 0 else 1),
        grid_spec=grid_spec,
        out_shape=jax.ShapeDtypeStruct((B, G, S, LANE), BF16),
        compiler_params=_params(2),
        name="pool",
    )(halves, u, band, pool_w.astype(BF16), pool_scale.reshape(G, 1, LANE).astype(F32))


def _hg_gates(q_ref, f_ref, v_ref, r0, lb):
    rows = pl.ds(r0, HG_GROUP)
    q = q_ref[0, 0, rows, :].astype(F32)
    q = q * jax.nn.sigmoid(q)
    f = lb + (1.0 - lb) * jax.nn.sigmoid(f_ref[0, 0, rows, :].astype(F32))
    g = jnp.log(f)
    g_hi = g.astype(BF16)
    g_lo = (g - g_hi.astype(F32)).astype(BF16)
    return q, 1.0 - f, v_ref[0, 0, rows, :], g_hi, g_lo


def _chunk_rows(rows):
    return jnp.concatenate([jnp.broadcast_to(r, (HG_CHUNK, LANE)) for r in rows], axis=0)


def _hg_decays(q, k, b, *, reverse):
    C = HG_CHUNK
    chunks = range(HG_GROUP // C)
    end = [b[c * C + (0 if reverse else C - 1)][None] for c in chunks]
    mid = [b[c * C + (C // 2 if reverse else C // 2 - 1)][None] for c in chunks]
    b_mid = _chunk_rows(mid)
    q_mid = q * jnp.exp(b - b_mid)
    k_mid = k * jnp.exp(b_mid - b)
    q_dec = (q_mid * _chunk_rows([jnp.exp(m) for m in mid])).astype(BF16)
    k_st = (k_mid * _chunk_rows([jnp.exp(e - m) for e, m in zip(end, mid)])).astype(BF16)
    dec = [jnp.exp(e) for e in end]
    return q_dec, q_mid.astype(BF16), k_mid.astype(BF16), k_st, dec


def _hgrn_body(q_ref, ff_ref, fb_ref, v_ref, g_ref, lbf_ref, lbb_ref, gain_ref, tri_ref,
               o_ref, of_scr, ob_scr, *, seq, rows, unroll):
    C = HG_CHUNK
    R = HG_GROUP
    n_c = R // C
    n_groups = seq // R
    lb = (lbf_ref[0], lbb_ref[0])
    f_refs = (ff_ref, fb_ref)
    scr = (of_scr, ob_scr)
    ri = lax.broadcasted_iota(jnp.int32, (R, R), 0)
    ci = lax.broadcasted_iota(jnp.int32, (R, R), 1)
    same = (ri // C) == (ci // C)
    masks = (same & (ci <= ri), same & (ci >= ri))
    contract_last = (((1,), (1,)), ((), ()))
    contract_first = (((0,), (0,)), ((), ()))

    def step(j, carry):
        streams = []
        for d in range(2):
            for i in range(unroll):
                gi = j * unroll + i
                streams.append((d, pl.multiple_of((gi if d == 0 else n_groups - 1 - gi) * R, R)))
        gates = [_hg_gates(q_ref, f_refs[d], v_ref, r0, lb[d]) for d, r0 in streams]
        cums = [jnp.dot(tri_ref[d], g_hi, preferred_element_type=F32)
                + jnp.dot(tri_ref[d], g_lo, preferred_element_type=F32)
                for (d, _), (_, _, _, g_hi, g_lo) in zip(streams, gates)]
        ops = [_hg_decays(q, k, b, reverse=(d == 1)) for (d, _), (q, k, _, _, _), b in zip(streams, gates, cums)]
        scores = [lax.dot_general(q_in, k_in, contract_last, preferred_element_type=F32)
                  for _, q_in, k_in, _, _ in ops]
        kvs = [[lax.dot_general(v[c * C:(c + 1) * C], k_st[c * C:(c + 1) * C], contract_first,
                                preferred_element_type=F32) for c in range(n_c)]
               for (_, _, v, _, _), (_, _, _, k_st, _) in zip(gates, ops)]
        intra = [jnp.dot(jnp.where(masks[d], s, 0.0).astype(BF16), v, preferred_element_type=F32)
                 for (d, _), s, (_, _, v, _, _) in zip(streams, scores, gates)]
        states = list(carry)
        for n, (d, r0) in enumerate(streams):
            q_dec, dec = ops[n][0], ops[n][4]
            inter = [None] * n_c
            for c in (range(n_c) if d == 0 else reversed(range(n_c))):
                inter[c] = lax.dot_general(q_dec[c * C:(c + 1) * C], states[d].astype(BF16), contract_last,
                                           preferred_element_type=F32)
                states[d] = states[d] * dec[c] + kvs[n][c]
            scr[d][pl.ds(r0, R), :] = intra[n] + jnp.concatenate(inter, axis=0)
        return tuple(states)

    zero = jnp.zeros((LANE, LANE), F32)
    lax.fori_loop(0, n_groups // unroll, step, (zero, zero))

    def finish(i, carry):
        r = pl.ds(pl.multiple_of(i * rows, rows), rows)
        o = of_scr[r, :] + ob_scr[r, :]
        o = _rms(o, gain_ref[...])
        gate = g_ref[0, 0, r, :].astype(F32)
        o_ref[0, 0, r, :] = (o * (gate * jax.nn.sigmoid(gate))).astype(BF16)
        return carry

    lax.fori_loop(0, seq // rows, finish, 0)


def _hgrn(u, lb_f, lb_b, gain, n_pool):
    B, _, S, _ = u.shape
    H = lb_f.shape[0]
    C = HG_CHUNK
    R = HG_GROUP
    assert S % R == 0
    low = np.kron(np.eye(R // C), np.tril(np.ones((C, C)))).astype(np.float32)
    tri = jnp.asarray(np.stack([low, low.T]), BF16)
    rows = min(256, S)
    unroll = HG_UNROLL if (S // R) % HG_UNROLL == 0 else 1

    def slab(k):
        return pl.BlockSpec((1, 1, S, LANE), lambda b, h, k=k: (b, n_pool + k * H + h, 0, 0))

    head_vec = pl.BlockSpec((1, 1, LANE), lambda b, h: (h, 0, 0))
    return pl.pallas_call(
        functools.partial(_hgrn_body, seq=S, rows=rows, unroll=unroll),
        grid=(B, H),
        in_specs=[slab(0), slab(1), slab(2), slab(3), slab(4), head_vec, head_vec,
                  pl.BlockSpec((1, LANE), lambda b, h: (0, 0)),
                  pl.BlockSpec((2, R, R), lambda b, h: (0, 0, 0))],
        out_specs=pl.BlockSpec((1, 1, S, LANE), lambda b, h: (b, h, 0, 0)),
        out_shape=jax.ShapeDtypeStruct((B, H, S, LANE), BF16),
        scratch_shapes=[pltpu.VMEM((S, LANE), F32), pltpu.VMEM((S, LANE), F32)],
        compiler_params=_params(2),
        name="hgrn",
    )(u, u, u, u, u, lb_f, lb_b, gain, tri)


INFO_GATE = 0
INFO_EXPERT = 2
INFO_RANK = 4


def _outproj_body(mp_ref, mh_ref, x_ref, wo_ref, gain_ref, wrh_ref, wrl_ref, rb_ref, ls_ref,
                  x2_ref, xn_ref, info_ref, cnt_ref, *, n_groups, epg):
    tm = x_ref.shape[1]

    @pl.when((pl.program_id(0) == 0) & (pl.program_id(1) == 0))
    def _():
        cnt_ref[...] = jnp.zeros_like(cnt_ref)

    mix = jnp.concatenate([mp_ref[0, j] for j in range(mp_ref.shape[1])]
                          + [mh_ref[0, j] for j in range(mh_ref.shape[1])], axis=-1)
    x2 = x_ref[0] + jnp.dot(mix, wo_ref[...], preferred_element_type=F32)
    x2_ref[0] = x2
    xn = _rms(x2, gain_ref[...])
    for j in range(xn.shape[1] // LANE):
        xn_ref[pl.ds(j, tm, stride=xn.shape[1] // LANE), :] = xn[:, j * LANE:(j + 1) * LANE]

    xh = xn.astype(BF16)
    xl = (xn - xh.astype(F32)).astype(BF16)
    logits = (jnp.dot(xh, wrh_ref[...], preferred_element_type=F32)
              + jnp.dot(xl, wrh_ref[...], preferred_element_type=F32)
              + jnp.dot(xh, wrl_ref[...], preferred_element_type=F32)) + rb_ref[...]

    lane = lax.broadcasted_iota(jnp.int32, (tm, LANE), 1)
    neg = jnp.float32(-jnp.inf)
    big = jnp.int32(LANE)

    def top(vals):
        m = jnp.max(vals, axis=-1, keepdims=True)
        return m, jnp.min(jnp.where(vals == m, lane, big), axis=-1, keepdims=True)

    glog = jnp.where(lane < n_groups, logits, neg)
    gmax, grp = top(glog)
    grp_prob = 1.0 / jnp.sum(jnp.exp(glog - gmax), axis=-1, keepdims=True)
    e_lo = n_groups + grp * epg
    elog = jnp.where((lane >= e_lo) & (lane < e_lo + epg), logits, neg)
    v1, i1 = top(elog)
    v2, i2 = top(jnp.where(lane == i1, neg, elog))
    e21 = jnp.exp(v2 - v1)
    gate1 = grp_prob / (1.0 + e21)
    gate2 = grp_prob * e21 / (1.0 + e21)

    hot1 = lane == i1
    hot2 = lane == i2
    onehot = jnp.where(hot1 | hot2, 1.0, 0.0)
    before = jnp.dot(ls_ref[...], onehot.astype(BF16), preferred_element_type=F32) + cnt_ref[...]
    rank1 = jnp.sum(jnp.where(hot1, before, 0.0), axis=-1, keepdims=True)
    rank2 = jnp.sum(jnp.where(hot2, before, 0.0), axis=-1, keepdims=True)
    cnt_ref[...] = cnt_ref[...] + jnp.sum(onehot, axis=0, keepdims=True)

    info = jnp.zeros((tm, LANE), F32)
    for k, col in ((INFO_GATE, gate1), (INFO_GATE + 1, gate2),
                   (INFO_EXPERT, (i1 - n_groups).astype(F32)), (INFO_EXPERT + 1, (i2 - n_groups).astype(F32)),
                   (INFO_RANK, rank1), (INFO_RANK + 1, rank2)):
        info = jnp.where(lane == k, col, info)
    info_ref[0] = info


def _outproj(mixp, mixh, x, w_out, gain, wr_hi, wr_lo, rbias, tm, n_groups, epg):
    B, S, D = x.shape
    n_sl = D // LANE
    lstrict = jnp.asarray(np.tril(np.ones((tm, tm), np.float32), -1), BF16)
    const = lambda b, i: (0, 0)
    return pl.pallas_call(
        functools.partial(_outproj_body, n_groups=n_groups, epg=epg),
        grid=(B, S // tm),
        in_specs=[
            pl.BlockSpec((1, mixp.shape[1], tm, LANE), lambda b, i: (b, 0, i, 0)),
            pl.BlockSpec((1, mixh.shape[1], tm, LANE), lambda b, i: (b, 0, i, 0)),
            pl.BlockSpec((1, tm, D), lambda b, i: (b, i, 0)),
            pl.BlockSpec(w_out.shape, const),
            pl.BlockSpec((1, D), const),
            pl.BlockSpec((D, LANE), const),
            pl.BlockSpec((D, LANE), const),
            pl.BlockSpec((1, LANE), const),
            pl.BlockSpec((tm, tm), const),
        ],
        out_specs=[
            pl.BlockSpec((1, tm, D), lambda b, i: (b, i, 0)),
            pl.BlockSpec((tm * n_sl, LANE), lambda b, i: (b * (S // tm) + i, 0)),
            pl.BlockSpec((1, tm, LANE), lambda b, i: (b, i, 0)),
            pl.BlockSpec((1, LANE), const),
        ],
        out_shape=[
            jax.ShapeDtypeStruct((B, S, D), F32),
            jax.ShapeDtypeStruct((B * S * n_sl, LANE), F32),
            jax.ShapeDtypeStruct((B, S, LANE), F32),
            jax.ShapeDtypeStruct((1, LANE), F32),
        ],
        compiler_params=_params(2),
        name="outproj",
    )(mixp, mixh, x, w_out, gain, wr_hi, wr_lo, rbias, lstrict)


def _start_rows(idx_ref, first, n, src_ref, dst_ref, sem, rows_per):
    for r in range(n):
        s = pl.multiple_of(idx_ref[first + r] * rows_per, rows_per)
        pltpu.make_async_copy(src_ref.at[pl.ds(s, rows_per)], dst_ref.at[pl.ds(r * rows_per, rows_per)],
                              sem).start(priority=r % 2)


def _wait_rows(n, src_ref, dst_ref, sem, rows_per):
    total = n * rows_per
    pltpu.make_async_copy(src_ref.at[pl.ds(0, total)], dst_ref.at[pl.ds(0, total)], sem).wait()


def _slab_rows(ref, n, n_sl):
    return jnp.concatenate([ref[pl.ds(j, n, stride=n_sl), :] for j in range(n_sl)], axis=-1)


def _experts_body(bexp_ref, tok_ref, nxt_ref, xn_ref, w1a_ref, w3a_ref, w2a_ref, w1b_ref, w3b_ref, w2b_ref,
                  y_ref, buf_a, buf_b, sem, *, n_sl):
    i = pl.program_id(0)
    blk = EXPERT_ROWS

    def mlp(buf, w1_ref, w3_ref, w2_ref, half):
        xb = _slab_rows(buf, blk, n_sl).astype(BF16)
        a = jnp.dot(xb, w1_ref[0], preferred_element_type=F32)
        h = (a * jax.nn.sigmoid(a)) * jnp.dot(xb, w3_ref[0], preferred_element_type=F32)
        y = jnp.dot(h.astype(BF16), w2_ref[0], preferred_element_type=F32)
        for j in range(n_sl):
            y_ref[pl.ds(half * blk * n_sl + j, blk, stride=n_sl), :] = y[:, j * LANE:(j + 1) * LANE]

    @pl.when(i == 0)
    def _():
        _start_rows(tok_ref, 0, blk, xn_ref, buf_a, sem.at[0], n_sl)

    _start_rows(tok_ref, blk, blk, xn_ref, buf_b, sem.at[1], n_sl)
    _wait_rows(blk, xn_ref, buf_a, sem.at[0], n_sl)
    mlp(buf_a, w1a_ref, w3a_ref, w2a_ref, 0)
    _start_rows(nxt_ref, 0, blk, xn_ref, buf_a, sem.at[0], n_sl)
    _wait_rows(blk, xn_ref, buf_b, sem.at[1], n_sl)
    mlp(buf_b, w1b_ref, w3b_ref, w2b_ref, 1)

    @pl.when(i == pl.num_programs(0) - 1)
    def _():
        _wait_rows(blk, xn_ref, buf_a, sem.at[0], n_sl)


def _experts(block_expert, slot_tok, xn_rows, w1, w3, w2, n_sl):
    n_blocks = block_expert.shape[0]
    assert n_blocks % 2 == 0
    steps = n_blocks // 2
    blk = EXPERT_ROWS
    _, D, DE = w1.shape

    def wspec(shape, half):
        return pl.BlockSpec(shape, lambda i, be, half=half: (be[2 * i + half], 0, 0))

    grid_spec = pltpu.PrefetchScalarGridSpec(
        num_scalar_prefetch=1,
        grid=(steps,),
        in_specs=[
            pl.BlockSpec((2 * blk,), lambda i, be: (i,), memory_space=pltpu.SMEM),
            pl.BlockSpec((blk,), lambda i, be: (jnp.minimum(2 * i + 2, n_blocks - 2),), memory_space=pltpu.SMEM),
            pl.BlockSpec(memory_space=pl.ANY),
            wspec((1, D, DE), 0), wspec((1, D, DE), 0), wspec((1, DE, D), 0),
            wspec((1, D, DE), 1), wspec((1, D, DE), 1), wspec((1, DE, D), 1),
        ],
        out_specs=pl.BlockSpec((2 * blk * n_sl, LANE), lambda i, be: (i, 0)),
        scratch_shapes=[pltpu.VMEM((blk * n_sl, LANE), F32), pltpu.VMEM((blk * n_sl, LANE), F32),
                        pltpu.SemaphoreType.DMA((2,))],
    )
    return pl.pallas_call(
        functools.partial(_experts_body, n_sl=n_sl),
        grid_spec=grid_spec,
        out_shape=jax.ShapeDtypeStruct((n_blocks * blk * n_sl, LANE), F32),
        compiler_params=_params(1),
        name="experts",
    )(block_expert, slot_tok, slot_tok, xn_rows, w1, w3, w2, w1, w3, w2)


def _final_body(d1_ref, d2_ref, n1_ref, n2_ref, x2_ref, info_ref, gain_ref, y_ref, o_ref,
                a1, a2, b1, b2, sem, *, n_sl, tg):
    i = pl.program_id(0)

    def combine(buf1, buf2, half):
        rows = pl.ds(half * tg, tg)
        info = info_ref[rows, :]
        moe = (_slab_rows(buf1, tg, n_sl) * info[:, INFO_GATE:INFO_GATE + 1]
               + _slab_rows(buf2, tg, n_sl) * info[:, INFO_GATE + 1:INFO_GATE + 2])
        o_ref[rows, :] = _rms(x2_ref[rows, :] + moe, gain_ref[...])

    def start(i1_ref, i2_ref, first, buf1, buf2, s):
        _start_rows(i1_ref, first, tg, y_ref, buf1, sem.at[s], n_sl)
        _start_rows(i2_ref, first, tg, y_ref, buf2, sem.at[s], n_sl)

    def wait(buf1, buf2, s):
        _wait_rows(tg, y_ref, buf1, sem.at[s], n_sl)
        _wait_rows(tg, y_ref, buf2, sem.at[s], n_sl)

    @pl.when(i == 0)
    def _():
        start(d1_ref, d2_ref, 0, a1, a2, 0)

    start(d1_ref, d2_ref, tg, b1, b2, 1)
    wait(a1, a2, 0)
    combine(a1, a2, 0)
    start(n1_ref, n2_ref, 0, a1, a2, 0)
    wait(b1, b2, 1)
    combine(b1, b2, 1)

    @pl.when(i == pl.num_programs(0) - 1)
    def _():
        wait(a1, a2, 0)


def _final(dest1, dest2, x2, info, gain, y_rows, tg, n_sl):
    T, D = x2.shape
    assert T % (2 * tg) == 0
    steps = T // (2 * tg)
    cur = lambda i: (i,)
    nxt = lambda i: (jnp.minimum(2 * i + 2, 2 * steps - 1),)
    buf = pltpu.VMEM((tg * n_sl, LANE), F32)
    return pl.pallas_call(
        functools.partial(_final_body, n_sl=n_sl, tg=tg),
        grid=(steps,),
        in_specs=[
            pl.BlockSpec((2 * tg,), cur, memory_space=pltpu.SMEM),
            pl.BlockSpec((2 * tg,), cur, memory_space=pltpu.SMEM),
            pl.BlockSpec((tg,), nxt, memory_space=pltpu.SMEM),
            pl.BlockSpec((tg,), nxt, memory_space=pltpu.SMEM),
            pl.BlockSpec((2 * tg, D), lambda i: (i, 0)),
            pl.BlockSpec((2 * tg, LANE), lambda i: (i, 0)),
            pl.BlockSpec((1, D), lambda i: (0, 0)),
            pl.BlockSpec(memory_space=pl.ANY),
        ],
        out_specs=pl.BlockSpec((2 * tg, D), lambda i: (i, 0)),
        out_shape=jax.ShapeDtypeStruct((T, D), F32),
        scratch_shapes=[buf, buf, buf, buf, pltpu.SemaphoreType.DMA((2,))],
        compiler_params=_params(1),
        name="final",
    )(dest1, dest2, dest1, dest2, x2, info, gain, y_rows)


def _tile(n, pref):
    return pref if n % pref == 0 else n


def _layer(x, p):
    B, S, D = x.shape
    T = B * S
    n_sl = D // LANE
    G = p["pool_w"].shape[0]
    H = p["lb_f"].shape[0]
    E = p["w1"].shape[0]
    n_groups = p["n_groups"]
    tm = _tile(S, 512)

    u = _inproj(x, p["norm_mix"], p["w_in"], tm)
    mixp = _pool(u, p["pool_w"], p["pool_scale"])
    mixh = _hgrn(u, p["lb_f"], p["lb_b"], p["hg_gain"], G)
    x2, xn_rows, info, counts = _outproj(mixp, mixh, x, p["w_out"], p["norm_ffn"], p["wr_hi"], p["wr_lo"],
                                         p["rbias"], tm, n_groups, E // n_groups)

    blk = EXPERT_ROWS
    A = T * TOP_K
    n_blocks = -(-A // blk) + E
    n_blocks += n_blocks % 2
    cnt = counts[0, n_groups:n_groups + E].astype(jnp.int32)
    padded = (cnt + blk - 1) // blk * blk
    pend = jnp.cumsum(padded)
    pstart = pend - padded
    info2 = info.reshape(T, LANE)
    eid = info2[:, INFO_EXPERT:INFO_EXPERT + TOP_K].astype(jnp.int32)
    rank = info2[:, INFO_RANK:INFO_RANK + TOP_K].astype(jnp.int32)
    dest = pstart[eid] + rank
    tok = jnp.broadcast_to(jnp.arange(T, dtype=jnp.int32)[:, None], (T, TOP_K))
    slot_tok = jnp.zeros((n_blocks * blk,), jnp.int32).at[dest.reshape(-1)].set(
        tok.reshape(-1), unique_indices=True, mode="promise_in_bounds")
    first_slot = jnp.arange(n_blocks, dtype=jnp.int32) * blk
    block_expert = jnp.minimum(jnp.sum(pend[None, :] <= first_slot[:, None], axis=1), E - 1).astype(jnp.int32)

    y_rows = _experts(block_expert, slot_tok, xn_rows, p["w1"], p["w3"], p["w2"], n_sl)
    tg = 128 if T % 256 == 0 else T // 2
    out = _final(dest[:, 0], dest[:, 1], x2.reshape(T, D), info2, p["norm_final"], y_rows, tg, n_sl)
    return out.reshape(B, S, D)


def kernel(x_prompt, x_sample, w_in, w_out, pool_w, pool_scale, hg_lb_fwd, hg_lb_bwd, hg_norm_gain, norm_mix, norm_ffn, router_group_w, router_group_b, router_expert_w, router_expert_b, expert_w1, expert_w3, expert_w2, norm_final):
    depth = w_in.shape[0]
    assert depth == 1, "the final norm is fused into the last layer's combine kernel; one layer supported"
    D = w_in.shape[1]
    hg_width = hg_lb_fwd.shape[1]
    dv = hg_norm_gain.shape[1]
    pg = pool_w.shape[2]
    assert dv == LANE and pg == LANE and D % LANE == 0
    H = hg_width // dv
    n_groups = router_group_w.shape[-1]
    E = router_expert_w.shape[-1]
    assert n_groups + E <= LANE

    lb_f = jnp.cumsum(jax.nn.softmax(hg_lb_fwd.astype(F32), axis=0), axis=0)
    lb_b = jnp.cumsum(jax.nn.softmax(hg_lb_bwd.astype(F32), axis=0), axis=0)
    l = 0
    wr = jnp.concatenate([router_group_w[l], router_expert_w[l]], axis=1).astype(F32)
    wr = jnp.pad(wr, ((0, 0), (0, LANE - wr.shape[1])))
    wr_hi = wr.astype(BF16)
    rbias = jnp.concatenate([router_group_b[l], router_expert_b[l]]).astype(F32)
    p = dict(
        n_groups=n_groups,
        w_in=w_in[l].astype(BF16), w_out=w_out[l].astype(BF16),
        pool_w=pool_w[l], pool_scale=pool_scale[l],
        lb_f=lb_f[l].reshape(H, 1, dv), lb_b=lb_b[l].reshape(H, 1, dv), hg_gain=hg_norm_gain[l].reshape(1, dv).astype(F32),
        norm_mix=norm_mix[l].reshape(1, D).astype(F32), norm_ffn=norm_ffn[l].reshape(1, D).astype(F32),
        norm_final=norm_final.reshape(1, D).astype(F32),
        wr_hi=wr_hi, wr_lo=(wr - wr_hi.astype(F32)).astype(BF16),
        rbias=jnp.pad(rbias, (0, LANE - rbias.shape[0])).reshape(1, LANE),
        w1=expert_w1[l].astype(BF16), w3=expert_w3[l].astype(BF16), w2=expert_w2[l].astype(BF16),
    )
    return (_layer(x_prompt, p), _layer(x_sample, p))
```

```python
import functools

import jax
import jax.numpy as jnp
import numpy as np
from jax import lax
from jax.experimental import pallas as pl
from jax.experimental.pallas import tpu as pltpu

F32 = jnp.float32
BF16 = jnp.bfloat16

EPS = 1e-6
POOL_WINDOWS = (2, 4, 8, 16)
TOP_K = 2

LANE = 128
SUBLANE = 8
VMEM_LIMIT = 56 * 1024 * 1024

HG_CHUNK = 32
HG_GROUP = 128
HG_UNROLL = 2
POOL_ROWS = 128
POOL_UNROLL = 4
EXPERT_ROWS = 256


def _params(n_axes):
    return pltpu.CompilerParams(dimension_semantics=("arbitrary",) * n_axes, vmem_limit_bytes=VMEM_LIMIT)


def _rms(x, gain):
    return x * lax.rsqrt(jnp.mean(x * x, axis=-1, keepdims=True) + EPS) * gain


def _inproj_body(x_ref, gain_ref, w_ref, u_ref, *, nc):
    n = _rms(x_ref[0], gain_ref[...]).astype(BF16)
    per = nc // LANE
    for c in range(w_ref.shape[1] // nc):
        r = jnp.dot(n, w_ref[:, c * nc:(c + 1) * nc], preferred_element_type=F32)
        for j in range(per):
            u_ref[0, c * per + j] = r[:, j * LANE:(j + 1) * LANE].astype(BF16)


def _inproj(x, gain, w, tm):
    B, S, D = x.shape
    cols = w.shape[1]
    return pl.pallas_call(
        functools.partial(_inproj_body, nc=4 * LANE),
        grid=(B, S // tm),
        in_specs=[
            pl.BlockSpec((1, tm, D), lambda b, i: (b, i, 0)),
            pl.BlockSpec((1, D), lambda b, i: (0, 0)),
            pl.BlockSpec((D, cols), lambda b, i: (0, 0)),
        ],
        out_specs=pl.BlockSpec((1, cols // LANE, tm, LANE), lambda b, i: (b, 0, i, 0)),
        out_shape=jax.ShapeDtypeStruct((B, cols // LANE, S, LANE), BF16),
        compiler_params=_params(2),
        name="inproj",
    )(x, gain, w)


def _band_matrices(rows):
    t = np.arange(rows)[:, None]
    s = np.arange(rows)[None, :]
    out = np.zeros((len(POOL_WINDOWS), 3, rows, rows), np.float32)
    for gi, w in enumerate(POOL_WINDOWS):
        for k, shift in enumerate((-rows, 0, rows)):
            pos = s + shift
            out[gi, k] = (pos >= t - w // 2) & (pos < t + w // 2)
    return out


def _pool_body(half_ref, u_ref, band_ref, pw_ref, sc_ref, o_ref, *, seq, rows, unroll):
    nt = seq // rows
    h = half_ref[pl.program_id(1)]

    def window_sum(i):
        r0 = pl.multiple_of(i * rows, rows)
        rp = pl.multiple_of(jnp.maximum(i - 1, 0) * rows, rows)
        rn = pl.multiple_of(jnp.minimum(i + 1, nt - 1) * rows, rows)
        xc = u_ref[0, 0, pl.ds(r0, rows), :]
        s = jnp.dot(band_ref[0, 1], xc, preferred_element_type=F32)
        sp = jnp.dot(band_ref[0, 0], u_ref[0, 0, pl.ds(rp, rows), :], preferred_element_type=F32)
        sn = jnp.dot(band_ref[0, 2], u_ref[0, 0, pl.ds(rn, rows), :], preferred_element_type=F32)
        return r0, xc, s + jnp.where(i > 0, sp, 0.0) + jnp.where(i < nt - 1, sn, 0.0)

    def pooled(r0, xc, s):
        t = r0 + lax.broadcasted_iota(jnp.int32, (rows, LANE), 0)
        cnt = (jnp.minimum(t + h, seq) - jnp.maximum(t - h, 0)).astype(F32)
        return (s / cnt - xc.astype(F32)).astype(BF16)

    def tiles(j, carry):
        sums = [window_sum(j * unroll + k) for k in range(unroll)]
        pools = [pooled(*a) for a in sums]
        ys = [jnp.dot(pv, pw_ref[0], preferred_element_type=F32) * sc_ref[0] for pv in pools]
        for (r0, _, _), y in zip(sums, ys):
            o_ref[0, 0, pl.ds(r0, rows), :] = y.astype(BF16)
        return carry

    lax.fori_loop(0, nt // unroll, tiles, 0)


def _pool(u, pool_w, pool_scale):
    B, _, S, _ = u.shape
    G = pool_w.shape[0]
    rows = min(POOL_ROWS, S)
    band = jnp.asarray(_band_matrices(rows), BF16)
    halves = jnp.asarray([w // 2 for w in POOL_WINDOWS], jnp.int32)
    grid_spec = pltpu.PrefetchScalarGridSpec(
        num_scalar_prefetch=1,
        grid=(B, G),
        in_specs=[
            pl.BlockSpec((1, 1, S, LANE), lambda b, g, h: (b, g, 0, 0)),
            pl.BlockSpec((1, 3, rows, rows), lambda b, g, h: (g, 0, 0, 0)),
            pl.BlockSpec((1, LANE, LANE), lambda b, g, h: (g, 0, 0)),
            pl.BlockSpec((1, 1, LANE), lambda b, g, h: (g, 0, 0)),
        ],
        out_specs=pl.BlockSpec((1, 1, S, LANE), lambda b, g, h: (b, g, 0, 0)),
    )
    return pl.pallas_call(
        functools.partial(_pool_body, seq=S, rows=rows, unroll=POOL_UNROLL if (S // rows) % POOL_UNROLL == 0 else 1),
        grid_spec=grid_spec,
        out_shape=jax.ShapeDtypeStruct((B, G, S, LANE), BF16),
        compiler_params=_params(2),
        name="pool",
    )(halves, u, band, pool_w.astype(BF16), pool_scale.reshape(G, 1, LANE).astype(F32))


def _hg_gates(q_ref, f_ref, v_ref, r0, lb):
    rows = pl.ds(r0, HG_GROUP)
    q = q_ref[0, 0, rows, :].astype(F32)
    q = q * jax.nn.sigmoid(q)
    f = lb + (1.0 - lb) * jax.nn.sigmoid(f_ref[0, 0, rows, :].astype(F32))
    g = jnp.log(f)
    g_hi = g.astype(BF16)
    g_lo = (g - g_hi.astype(F32)).astype(BF16)
    return q, 1.0 - f, v_ref[0, 0, rows, :], g_hi, g_lo


def _chunk_rows(rows):
    return jnp.concatenate([jnp.broadcast_to(r, (HG_CHUNK, LANE)) for r in rows], axis=0)


def _hg_decays(q, k, b, *, reverse):
    C = HG_CHUNK
    chunks = range(HG_GROUP // C)
    end = [b[c * C + (0 if reverse else C - 1)][None] for c in chunks]
    mid = [b[c * C + (C // 2 if reverse else C // 2 - 1)][None] for c in chunks]
    b_mid = _chunk_rows(mid)
    q_mid = q * jnp.exp(b - b_mid)
    k_mid = k * jnp.exp(b_mid - b)
    q_dec = (q_mid * _chunk_rows([jnp.exp(m) for m in mid])).astype(BF16)
    k_st = (k_mid * _chunk_rows([jnp.exp(e - m) for e, m in zip(end, mid)])).astype(BF16)
    dec = [jnp.exp(e) for e in end]
    return q_dec, q_mid.astype(BF16), k_mid.astype(BF16), k_st, dec


def _hgrn_body(q_ref, ff_ref, fb_ref, v_ref, g_ref, lbf_ref, lbb_ref, gain_ref, tri_ref,
               o_ref, of_scr, ob_scr, *, seq, rows, unroll):
    C = HG_CHUNK
    R = HG_GROUP
    n_c = R // C
    n_groups = seq // R
    lb = (lbf_ref[0], lbb_ref[0])
    f_refs = (ff_ref, fb_ref)
    scr = (of_scr, ob_scr)
    ri = lax.broadcasted_iota(jnp.int32, (R, R), 0)
    ci = lax.broadcasted_iota(jnp.int32, (R, R), 1)
    same = (ri // C) == (ci // C)
    masks = (same & (ci <= ri), same & (ci >= ri))
    contract_last = (((1,), (1,)), ((), ()))
    contract_first = (((0,), (0,)), ((), ()))

    def step(j, carry):
        streams = []
        for d in range(2):
            for i in range(unroll):
                gi = j * unroll + i
                streams.append((d, pl.multiple_of((gi if d == 0 else n_groups - 1 - gi) * R, R)))
        gates = [_hg_gates(q_ref, f_refs[d], v_ref, r0, lb[d]) for d, r0 in streams]
        cums = [jnp.dot(tri_ref[d], g_hi, preferred_element_type=F32)
                + jnp.dot(tri_ref[d], g_lo, preferred_element_type=F32)
                for (d, _), (_, _, _, g_hi, g_lo) in zip(streams, gates)]
        ops = [_hg_decays(q, k, b, reverse=(d == 1)) for (d, _), (q, k, _, _, _), b in zip(streams, gates, cums)]
        scores = [lax.dot_general(q_in, k_in, contract_last, preferred_element_type=F32)
                  for _, q_in, k_in, _, _ in ops]
        kvs = [[lax.dot_general(v[c * C:(c + 1) * C], k_st[c * C:(c + 1) * C], contract_first,
                                preferred_element_type=F32) for c in range(n_c)]
               for (_, _, v, _, _), (_, _, _, k_st, _) in zip(gates, ops)]
        intra = [jnp.dot(jnp.where(masks[d], s, 0.0).astype(BF16), v, preferred_element_type=F32)
                 for (d, _), s, (_, _, v, _, _) in zip(streams, scores, gates)]
        states = list(carry)
        for n, (d, r0) in enumerate(streams):
            q_dec, dec = ops[n][0], ops[n][4]
            inter = [None] * n_c
            for c in (range(n_c) if d == 0 else reversed(range(n_c))):
                inter[c] = lax.dot_general(q_dec[c * C:(c + 1) * C], states[d].astype(BF16), contract_last,
                                           preferred_element_type=F32)
                states[d] = states[d] * dec[c] + kvs[n][c]
            scr[d][pl.ds(r0, R), :] = intra[n] + jnp.concatenate(inter, axis=0)
        return tuple(states)

    zero = jnp.zeros((LANE, LANE), F32)
    lax.fori_loop(0, n_groups // unroll, step, (zero, zero))

    def finish(i, carry):
        r = pl.ds(pl.multiple_of(i * rows, rows), rows)
        o = of_scr[r, :] + ob_scr[r, :]
        o = _rms(o, gain_ref[...])
        gate = g_ref[0, 0, r, :].astype(F32)
        o_ref[0, 0, r, :] = (o * (gate * jax.nn.sigmoid(gate))).astype(BF16)
        return carry

    lax.fori_loop(0, seq // rows, finish, 0)


def _hgrn(u, lb_f, lb_b, gain, n_pool):
    B, _, S, _ = u.shape
    H = lb_f.shape[0]
    C = HG_CHUNK
    R = HG_GROUP
    assert S % R == 0
    low = np.kron(np.eye(R // C), np.tril(np.ones((C, C)))).astype(np.float32)
    tri = jnp.asarray(np.stack([low, low.T]), BF16)
    rows = min(256, S)
    unroll = HG_UNROLL if (S // R) % HG_UNROLL == 0 else 1

    def slab(k):
        return pl.BlockSpec((1, 1, S, LANE), lambda b, h, k=k: (b, n_pool + k * H + h, 0, 0))

    head_vec = pl.BlockSpec((1, 1, LANE), lambda b, h: (h, 0, 0))
    return pl.pallas_call(
        functools.partial(_hgrn_body, seq=S, rows=rows, unroll=unroll),
        grid=(B, H),
        in_specs=[slab(0), slab(1), slab(2), slab(3), slab(4), head_vec, head_vec,
                  pl.BlockSpec((1, LANE), lambda b, h: (0, 0)),
                  pl.BlockSpec((2, R, R), lambda b, h: (0, 0, 0))],
        out_specs=pl.BlockSpec((1, 1, S, LANE), lambda b, h: (b, h, 0, 0)),
        out_shape=jax.ShapeDtypeStruct((B, H, S, LANE), BF16),
        scratch_shapes=[pltpu.VMEM((S, LANE), F32), pltpu.VMEM((S, LANE), F32)],
        compiler_params=_params(2),
        name="hgrn",
    )(u, u, u, u, u, lb_f, lb_b, gain, tri)


INFO_GATE = 0
INFO_LPOS = 2

TAB_COUNT = 0
TAB_BEFORE = 1
TAB_LSTART = 2

MOE_CHUNK = 8


def _outproj_body(mp_ref, mh_ref, x_ref, wo_ref, gain_ref, wrh_ref, wrl_ref, rb_ref, ls_ref, us_ref,
                  x2_ref, xn_ref, info_ref, tab_ref, before_ref, *, n_groups, epg):
    tm = x_ref.shape[1]

    @pl.when((pl.program_id(0) == 0) & (pl.program_id(1) == 0))
    def _():
        before_ref[...] = jnp.zeros_like(before_ref)

    mix = jnp.concatenate([mp_ref[0, j] for j in range(mp_ref.shape[1])]
                          + [mh_ref[0, j] for j in range(mh_ref.shape[1])], axis=-1)
    x2 = x_ref[0] + jnp.dot(mix, wo_ref[...], preferred_element_type=F32)
    x2_ref[0] = x2
    xn = _rms(x2, gain_ref[...])
    for j in range(xn.shape[1] // LANE):
        xn_ref[pl.ds(j, tm, stride=xn.shape[1] // LANE), :] = xn[:, j * LANE:(j + 1) * LANE]

    xh = xn.astype(BF16)
    xl = (xn - xh.astype(F32)).astype(BF16)
    logits = (jnp.dot(xh, wrh_ref[...], preferred_element_type=F32)
              + jnp.dot(xl, wrh_ref[...], preferred_element_type=F32)
              + jnp.dot(xh, wrl_ref[...], preferred_element_type=F32)) + rb_ref[...]

    lane = lax.broadcasted_iota(jnp.int32, (tm, LANE), 1)
    neg = jnp.float32(-jnp.inf)
    big = jnp.int32(LANE)

    def top(vals):
        m = jnp.max(vals, axis=-1, keepdims=True)
        return m, jnp.min(jnp.where(vals == m, lane, big), axis=-1, keepdims=True)

    glog = jnp.where(lane < n_groups, logits, neg)
    gmax, grp = top(glog)
    grp_prob = 1.0 / jnp.sum(jnp.exp(glog - gmax), axis=-1, keepdims=True)
    e_lo = n_groups + grp * epg
    elog = jnp.where((lane >= e_lo) & (lane < e_lo + epg), logits, neg)
    v1, i1 = top(elog)
    v2, i2 = top(jnp.where(lane == i1, neg, elog))
    e21 = jnp.exp(v2 - v1)
    gate1 = grp_prob / (1.0 + e21)
    gate2 = grp_prob * e21 / (1.0 + e21)

    hot1 = lane == i1
    hot2 = lane == i2
    onehot = jnp.where(hot1 | hot2, 1.0, 0.0)
    earlier = jnp.dot(ls_ref[...], onehot.astype(BF16), preferred_element_type=F32)
    chunks = jnp.ceil(jnp.sum(onehot, axis=0, keepdims=True) * (1.0 / MOE_CHUNK))
    chunks8 = jnp.broadcast_to(chunks, (SUBLANE, LANE))
    lstart = jnp.dot(chunks8.astype(BF16), us_ref[...], preferred_element_type=F32) * MOE_CHUNK
    pos = earlier + lstart[0:1]
    lpos1 = jnp.sum(jnp.where(hot1, pos, 0.0), axis=-1, keepdims=True)
    lpos2 = jnp.sum(jnp.where(hot2, pos, 0.0), axis=-1, keepdims=True)

    count = chunks8 * MOE_CHUNK
    row = lax.broadcasted_iota(jnp.int32, (SUBLANE, LANE), 0)
    tab_ref[...] = jnp.where(row == TAB_COUNT, count,
                             jnp.where(row == TAB_BEFORE, before_ref[...],
                                       jnp.where(row == TAB_LSTART, lstart, 0.0)))
    before_ref[...] = before_ref[...] + count

    info = jnp.zeros((tm, LANE), F32)
    for k, col in ((INFO_GATE, gate1), (INFO_GATE + 1, gate2), (INFO_LPOS, lpos1), (INFO_LPOS + 1, lpos2)):
        info = jnp.where(lane == k, col, info)
    info_ref[0] = info


def _outproj(mixp, mixh, x, w_out, gain, wr_hi, wr_lo, rbias, tm, n_groups, epg):
    B, S, D = x.shape
    n_sl = D // LANE
    lstrict = jnp.asarray(np.tril(np.ones((tm, tm), np.float32), -1), BF16)
    ustrict = jnp.asarray(np.triu(np.ones((LANE, LANE), np.float32), 1), BF16)
    const = lambda b, i: (0, 0)
    tile = lambda b, i: (b * (S // tm) + i, 0)
    return pl.pallas_call(
        functools.partial(_outproj_body, n_groups=n_groups, epg=epg),
        grid=(B, S // tm),
        in_specs=[
            pl.BlockSpec((1, mixp.shape[1], tm, LANE), lambda b, i: (b, 0, i, 0)),
            pl.BlockSpec((1, mixh.shape[1], tm, LANE), lambda b, i: (b, 0, i, 0)),
            pl.BlockSpec((1, tm, D), lambda b, i: (b, i, 0)),
            pl.BlockSpec(w_out.shape, const),
            pl.BlockSpec((1, D), const),
            pl.BlockSpec((D, LANE), const),
            pl.BlockSpec((D, LANE), const),
            pl.BlockSpec((1, LANE), const),
            pl.BlockSpec((tm, tm), const),
            pl.BlockSpec((LANE, LANE), const),
        ],
        out_specs=[
            pl.BlockSpec((1, tm, D), lambda b, i: (b, i, 0)),
            pl.BlockSpec((tm * n_sl, LANE), tile),
            pl.BlockSpec((1, tm, LANE), lambda b, i: (b, i, 0)),
            pl.BlockSpec((SUBLANE, LANE), tile),
        ],
        out_shape=[
            jax.ShapeDtypeStruct((B, S, D), F32),
            jax.ShapeDtypeStruct((B * S * n_sl, LANE), F32),
            jax.ShapeDtypeStruct((B, S, LANE), F32),
            jax.ShapeDtypeStruct((B * (S // tm) * SUBLANE, LANE), F32),
        ],
        scratch_shapes=[pltpu.VMEM((SUBLANE, LANE), F32)],
        compiler_params=_params(2),
        name="outproj",
    )(mixp, mixh, x, w_out, gain, wr_hi, wr_lo, rbias, lstrict, ustrict)


def _dispatch_body(p1_ref, p2_ref, xn_ref, xs_ref, *, tm, n_sl):
    xs_ref[...] = jnp.zeros_like(xs_ref)

    def copy(t, carry):
        row = xn_ref[pl.ds(pl.multiple_of(t * n_sl, n_sl), n_sl), :]
        xs_ref[pl.ds(pl.multiple_of(p1_ref[t] * n_sl, n_sl), n_sl), :] = row
        xs_ref[pl.ds(pl.multiple_of(p2_ref[t] * n_sl, n_sl), n_sl), :] = row
        return carry

    lax.fori_loop(0, tm, copy, 0, unroll=8)


def _dispatch(lpos1, lpos2, xn_rows, tm, cap, n_sl):
    n_tiles = lpos1.shape[0] // tm
    return pl.pallas_call(
        functools.partial(_dispatch_body, tm=tm, n_sl=n_sl),
        grid=(n_tiles,),
        in_specs=[
            pl.BlockSpec((tm,), lambda i: (i,), memory_space=pltpu.SMEM),
            pl.BlockSpec((tm,), lambda i: (i,), memory_space=pltpu.SMEM),
            pl.BlockSpec((tm * n_sl, LANE), lambda i: (i, 0)),
        ],
        out_specs=pl.BlockSpec((cap * n_sl, LANE), lambda i: (i, 0)),
        out_shape=jax.ShapeDtypeStruct((n_tiles * cap * n_sl, LANE), F32),
        compiler_params=_params(1),
        name="dispatch",
    )(lpos1, lpos2, xn_rows)


def _start_rows(idx_ref, first, n, src_ref, dst_ref, sem, rows_per):
    for r in range(n):
        s = pl.multiple_of(idx_ref[first + r] * rows_per, rows_per)
        pltpu.make_async_copy(src_ref.at[pl.ds(s, rows_per)], dst_ref.at[pl.ds(r * rows_per, rows_per)], sem).start()


def _wait_rows(n, src_ref, dst_ref, sem, rows_per):
    total = n * rows_per
    pltpu.make_async_copy(src_ref.at[pl.ds(0, total)], dst_ref.at[pl.ds(0, total)], sem).wait()


def _index_stride(n):
    return max(LANE, pl.next_power_of_2(n))


def _pad_lists(idx, n):
    return jnp.pad(idx.reshape(-1, n), ((0, 0), (0, _index_stride(n) - n))).reshape(-1)


def _slab_rows(ref, n, n_sl):
    return jnp.concatenate([ref[pl.ds(j, n, stride=n_sl), :] for j in range(n_sl)], axis=-1)


def _experts_body(bexp_ref, nused_ref, src_ref, nxt_ref, xs_ref, w1a_ref, w3a_ref, w2a_ref, w1b_ref, w3b_ref, w2b_ref,
                  y_ref, buf_a, buf_b, sem, *, n_sl):
    i = pl.program_id(0)
    blk = EXPERT_ROWS
    n_ch = blk // MOE_CHUNK
    piece = MOE_CHUNK * n_sl

    def mlp(buf, w1_ref, w3_ref, w2_ref, half):
        @pl.when(2 * i + half < nused_ref[0])
        def _():
            xb = _slab_rows(buf, blk, n_sl).astype(BF16)
            a = jnp.dot(xb, w1_ref[0], preferred_element_type=F32)
            h = (a * jax.nn.sigmoid(a)) * jnp.dot(xb, w3_ref[0], preferred_element_type=F32)
            y = jnp.dot(h.astype(BF16), w2_ref[0], preferred_element_type=F32)
            for j in range(n_sl):
                y_ref[pl.ds(half * blk * n_sl + j, blk, stride=n_sl), :] = y[:, j * LANE:(j + 1) * LANE]

        @pl.when(2 * i + half >= nused_ref[0])
        def _():
            y_ref[pl.ds(half * blk * n_sl, blk * n_sl), :] = jnp.zeros((blk * n_sl, LANE), F32)

    @pl.when(i == 0)
    def _():
        _start_rows(src_ref, 0, n_ch, xs_ref, buf_a, sem.at[0], piece)

    _start_rows(src_ref, _index_stride(n_ch), n_ch, xs_ref, buf_b, sem.at[1], piece)
    _wait_rows(n_ch, xs_ref, buf_a, sem.at[0], piece)
    mlp(buf_a, w1a_ref, w3a_ref, w2a_ref, 0)
    _start_rows(nxt_ref, 0, n_ch, xs_ref, buf_a, sem.at[0], piece)
    _wait_rows(n_ch, xs_ref, buf_b, sem.at[1], piece)
    mlp(buf_b, w1b_ref, w3b_ref, w2b_ref, 1)

    @pl.when(i == pl.num_programs(0) - 1)
    def _():
        _wait_rows(n_ch, xs_ref, buf_a, sem.at[0], piece)


def _experts(block_expert, n_used, chunk_src, xs_rows, w1, w3, w2, n_sl):
    n_blocks = block_expert.shape[0]
    assert n_blocks % 2 == 0
    steps = n_blocks // 2
    blk = EXPERT_ROWS
    n_ch = blk // MOE_CHUNK
    stride = _index_stride(n_ch)
    chunk_src = _pad_lists(chunk_src, n_ch)
    _, D, DE = w1.shape

    def wspec(shape, half):
        return pl.BlockSpec(shape, lambda i, be, nu, half=half: (be[2 * i + half], 0, 0))

    grid_spec = pltpu.PrefetchScalarGridSpec(
        num_scalar_prefetch=2,
        grid=(steps,),
        in_specs=[
            pl.BlockSpec((2 * stride,), lambda i, be, nu: (i,), memory_space=pltpu.SMEM),
            pl.BlockSpec((stride,), lambda i, be, nu: (jnp.minimum(2 * i + 2, n_blocks - 2),), memory_space=pltpu.SMEM),
            pl.BlockSpec(memory_space=pl.ANY),
            wspec((1, D, DE), 0), wspec((1, D, DE), 0), wspec((1, DE, D), 0),
            wspec((1, D, DE), 1), wspec((1, D, DE), 1), wspec((1, DE, D), 1),
        ],
        out_specs=pl.BlockSpec((2 * blk * n_sl, LANE), lambda i, be, nu: (i, 0)),
        scratch_shapes=[pltpu.VMEM((blk * n_sl, LANE), F32), pltpu.VMEM((blk * n_sl, LANE), F32),
                        pltpu.SemaphoreType.DMA((2,))],
    )
    return pl.pallas_call(
        functools.partial(_experts_body, n_sl=n_sl),
        grid_spec=grid_spec,
        out_shape=jax.ShapeDtypeStruct((n_blocks * blk * n_sl, LANE), F32),
        compiler_params=_params(1),
        name="experts",
    )(block_expert, n_used, chunk_src, chunk_src, xs_rows, w1, w3, w2, w1, w3, w2)


def _final_body(p1_ref, p2_ref, src_ref, nxt_ref, x2_ref, info_ref, gain_ref, y_ref, o_ref,
                ybuf_a, ybuf_b, tok1, tok2, sem, *, n_sl, tm, n_ch):
    i = pl.program_id(0)
    piece = MOE_CHUNK * n_sl

    def combine(ybuf, half):
        def unsort(t, carry):
            dst = pl.ds(pl.multiple_of(t * n_sl, n_sl), n_sl)
            tok1[dst, :] = ybuf[pl.ds(pl.multiple_of(p1_ref[half * tm + t] * n_sl, n_sl), n_sl), :]
            tok2[dst, :] = ybuf[pl.ds(pl.multiple_of(p2_ref[half * tm + t] * n_sl, n_sl), n_sl), :]
            return carry

        lax.fori_loop(0, tm, unsort, 0, unroll=8)
        rows = pl.ds(half * tm, tm)
        info = info_ref[rows, :]
        moe = (_slab_rows(tok1, tm, n_sl) * info[:, INFO_GATE:INFO_GATE + 1]
               + _slab_rows(tok2, tm, n_sl) * info[:, INFO_GATE + 1:INFO_GATE + 2])
        o_ref[rows, :] = _rms(x2_ref[rows, :] + moe, gain_ref[...])

    @pl.when(i == 0)
    def _():
        _start_rows(src_ref, 0, n_ch, y_ref, ybuf_a, sem.at[0], piece)

    _start_rows(src_ref, _index_stride(n_ch), n_ch, y_ref, ybuf_b, sem.at[1], piece)
    _wait_rows(n_ch, y_ref, ybuf_a, sem.at[0], piece)
    combine(ybuf_a, 0)
    _start_rows(nxt_ref, 0, n_ch, y_ref, ybuf_a, sem.at[0], piece)
    _wait_rows(n_ch, y_ref, ybuf_b, sem.at[1], piece)
    combine(ybuf_b, 1)

    @pl.when(i == pl.num_programs(0) - 1)
    def _():
        _wait_rows(n_ch, y_ref, ybuf_a, sem.at[0], piece)


def _final(lpos1, lpos2, y_src, x2, info, gain, y_rows, tm, cap, n_sl):
    T, D = x2.shape
    assert T % (2 * tm) == 0
    steps = T // (2 * tm)
    n_ch = cap // MOE_CHUNK
    stride = _index_stride(n_ch)
    y_src = _pad_lists(y_src, n_ch)
    cur = lambda i: (i,)
    nxt = lambda i: (jnp.minimum(2 * i + 2, 2 * steps - 1),)
    ybuf = pltpu.VMEM((cap * n_sl, LANE), F32)
    tbuf = pltpu.VMEM((tm * n_sl, LANE), F32)
    return pl.pallas_call(
        functools.partial(_final_body, n_sl=n_sl, tm=tm, n_ch=n_ch),
        grid=(steps,),
        in_specs=[
            pl.BlockSpec((2 * tm,), cur, memory_space=pltpu.SMEM),
            pl.BlockSpec((2 * tm,), cur, memory_space=pltpu.SMEM),
            pl.BlockSpec((2 * stride,), cur, memory_space=pltpu.SMEM),
            pl.BlockSpec((stride,), nxt, memory_space=pltpu.SMEM),
            pl.BlockSpec((2 * tm, D), lambda i: (i, 0)),
            pl.BlockSpec((2 * tm, LANE), lambda i: (i, 0)),
            pl.BlockSpec((1, D), lambda i: (0, 0)),
            pl.BlockSpec(memory_space=pl.ANY),
        ],
        out_specs=pl.BlockSpec((2 * tm, D), lambda i: (i, 0)),
        out_shape=jax.ShapeDtypeStruct((T, D), F32),
        scratch_shapes=[ybuf, ybuf, tbuf, tbuf, pltpu.SemaphoreType.DMA((2,))],
        compiler_params=_params(1),
        name="final",
    )(lpos1, lpos2, y_src, y_src, x2, info, gain, y_rows)


def _tile(n, pref):
    return pref if n % pref == 0 else n


def _run_of(starts, n):
    lead = starts.shape[:-1]
    marks = jnp.zeros(lead + (n + 1,), jnp.int32)
    if lead:
        marks = marks.at[jnp.arange(lead[0])[:, None], starts].add(1)
    else:
        marks = marks.at[starts].add(1)
    return jnp.maximum(jnp.cumsum(marks, axis=-1)[..., :n] - 1, 0)


def _layer(x, p):
    B, S, D = x.shape
    T = B * S
    n_sl = D // LANE
    G = p["pool_w"].shape[0]
    E = p["w1"].shape[0]
    n_groups = p["n_groups"]
    tm = _tile(S, 512)
    n_tiles = T // tm

    u = _inproj(x, p["norm_mix"], p["w_in"], tm)
    mixp = _pool(u, p["pool_w"], p["pool_scale"])
    mixh = _hgrn(u, p["lb_f"], p["lb_b"], p["hg_gain"], G)
    x2, xn_rows, info, tab = _outproj(mixp, mixh, x, p["w_out"], p["norm_ffn"], p["wr_hi"], p["wr_lo"],
                                      p["rbias"], tm, n_groups, E // n_groups)

    blk = EXPERT_ROWS
    ch = MOE_CHUNK
    cap = TOP_K * tm + E * ch
    n_blocks = -(-(T * TOP_K + n_tiles * E * (ch - 1)) // blk) + E
    n_blocks += n_blocks % 2
    tab = tab.reshape(n_tiles, SUBLANE, LANE)[:, :, n_groups:n_groups + E].astype(jnp.int32)
    count, before, lstart = tab[:, TAB_COUNT], tab[:, TAB_BEFORE], tab[:, TAB_LSTART]
    total = before[-1] + count[-1]
    padded = (total + blk - 1) // blk * blk
    pend = jnp.cumsum(padded)
    first_slot = jnp.arange(n_blocks, dtype=jnp.int32) * blk
    block_expert = jnp.minimum(jnp.sum(pend[None, :] <= first_slot[:, None], axis=1), E - 1).astype(jnp.int32)
    n_used = (pend[-1:] // blk).astype(jnp.int32)
    slot_chunk = ((pend - padded)[None, :] + before) // ch
    local_chunk = lstart // ch
    src_chunk = jnp.arange(n_tiles, dtype=jnp.int32)[:, None] * (cap // ch) + local_chunk

    n_slot_chunks = n_blocks * blk // ch
    slot_flat, src_flat = slot_chunk.T.reshape(-1), src_chunk.T.reshape(-1)
    run = _run_of(slot_flat, n_slot_chunks)
    chunk_src = jnp.clip(src_flat[run] + jnp.arange(n_slot_chunks, dtype=jnp.int32) - slot_flat[run],
                         0, n_tiles * (cap // ch) - 1)
    lrun = _run_of(local_chunk, cap // ch)
    j = jnp.arange(cap // ch, dtype=jnp.int32)[None, :]
    y_src = jnp.clip(jnp.take_along_axis(slot_chunk, lrun, axis=1) + j - jnp.take_along_axis(local_chunk, lrun, axis=1),
                     0, n_slot_chunks - 1).reshape(-1)

    info2 = info.reshape(T, LANE)
    lpos = info2[:, INFO_LPOS:INFO_LPOS + TOP_K].astype(jnp.int32)
    xs_rows = _dispatch(lpos[:, 0], lpos[:, 1], xn_rows, tm, cap, n_sl)
    y_rows = _experts(block_expert, n_used, chunk_src, xs_rows, p["w1"], p["w3"], p["w2"], n_sl)
    out = _final(lpos[:, 0], lpos[:, 1], y_src, x2.reshape(T, D), info2, p["norm_final"], y_rows, tm, cap, n_sl)
    return out.reshape(B, S, D)


def kernel(x_prompt, x_sample, w_in, w_out, pool_w, pool_scale, hg_lb_fwd, hg_lb_bwd, hg_norm_gain, norm_mix, norm_ffn, router_group_w, router_group_b, router_expert_w, router_expert_b, expert_w1, expert_w3, expert_w2, norm_final):
    depth = w_in.shape[0]
    assert depth == 1, "the final norm is fused into the last layer's combine kernel; one layer supported"
    D = w_in.shape[1]
    hg_width = hg_lb_fwd.shape[1]
    dv = hg_norm_gain.shape[1]
    pg = pool_w.shape[2]
    assert dv == LANE and pg == LANE and D % LANE == 0
    H = hg_width // dv
    n_groups = router_group_w.shape[-1]
    E = router_expert_w.shape[-1]
    assert n_groups + E <= LANE

    lb_f = jnp.cumsum(jax.nn.softmax(hg_lb_fwd.astype(F32), axis=0), axis=0)
    lb_b = jnp.cumsum(jax.nn.softmax(hg_lb_bwd.astype(F32), axis=0), axis=0)
    l = 0
    wr = jnp.concatenate([router_group_w[l], router_expert_w[l]], axis=1).astype(F32)
    wr = jnp.pad(wr, ((0, 0), (0, LANE - wr.shape[1])))
    wr_hi = wr.astype(BF16)
    rbias = jnp.concatenate([router_group_b[l], router_expert_b[l]]).astype(F32)
    p = dict(
        n_groups=n_groups,
        w_in=w_in[l].astype(BF16), w_out=w_out[l].astype(BF16),
        pool_w=pool_w[l], pool_scale=pool_scale[l],
        lb_f=lb_f[l].reshape(H, 1, dv), lb_b=lb_b[l].reshape(H, 1, dv), hg_gain=hg_norm_gain[l].reshape(1, dv).astype(F32),
        norm_mix=norm_mix[l].reshape(1, D).astype(F32), norm_ffn=norm_ffn[l].reshape(1, D).astype(F32),
        norm_final=norm_final.reshape(1, D).astype(F32),
        wr_hi=wr_hi, wr_lo=(wr - wr_hi.astype(F32)).astype(BF16),
        rbias=jnp.pad(rbias, (0, LANE - rbias.shape[0])).reshape(1, LANE),
        w1=expert_w1[l].astype(BF16), w3=expert_w3[l].astype(BF16), w2=expert_w2[l].astype(BF16),
    )
    return (_layer(x_prompt, p), _layer(x_sample, p))
```

```python
import functools

import jax
import jax.numpy as jnp
import numpy as np
from jax import lax
from jax.experimental import pallas as pl
from jax.experimental.pallas import tpu as pltpu

F32 = jnp.float32
BF16 = jnp.bfloat16

EPS = 1e-6
POOL_WINDOWS = (2, 4, 8, 16)
TOP_K = 2

LANE = 128
SUBLANE = 8
VMEM_LIMIT = 56 * 1024 * 1024

HG_CHUNK = 32
HG_GROUP = 128
HG_UNROLL = 2
POOL_ROWS = 128
POOL_UNROLL = 4
EXPERT_ROWS = 256


def _params(n_axes):
    return pltpu.CompilerParams(dimension_semantics=("arbitrary",) * n_axes, vmem_limit_bytes=VMEM_LIMIT)


def _rms(x, gain):
    return x * lax.rsqrt(jnp.mean(x * x, axis=-1, keepdims=True) + EPS) * gain


def _inproj_body(x_ref, gain_ref, w_ref, u_ref, *, nc):
    n = _rms(x_ref[0], gain_ref[...]).astype(BF16)
    per = nc // LANE
    for c in range(w_ref.shape[1] // nc):
        r = jnp.dot(n, w_ref[:, c * nc:(c + 1) * nc], preferred_element_type=F32)
        for j in range(per):
            u_ref[0, c * per + j] = r[:, j * LANE:(j + 1) * LANE].astype(BF16)


def _inproj(x, gain, w, tm):
    B, S, D = x.shape
    cols = w.shape[1]
    return pl.pallas_call(
        functools.partial(_inproj_body, nc=4 * LANE),
        grid=(B, S // tm),
        in_specs=[
            pl.BlockSpec((1, tm, D), lambda b, i: (b, i, 0)),
            pl.BlockSpec((1, D), lambda b, i: (0, 0)),
            pl.BlockSpec((D, cols), lambda b, i: (0, 0)),
        ],
        out_specs=pl.BlockSpec((1, cols // LANE, tm, LANE), lambda b, i: (b, 0, i, 0)),
        out_shape=jax.ShapeDtypeStruct((B, cols // LANE, S, LANE), BF16),
        compiler_params=_params(2),
        name="inproj",
    )(x, gain, w)


def _band_matrices(rows):
    t = np.arange(rows)[:, None]
    s = np.arange(rows)[None, :]
    out = np.zeros((len(POOL_WINDOWS), 3, rows, rows), np.float32)
    for gi, w in enumerate(POOL_WINDOWS):
        for k, shift in enumerate((-rows, 0, rows)):
            pos = s + shift
            out[gi, k] = (pos >= t - w // 2) & (pos < t + w // 2)
    return out


def _pool_body(half_ref, u_ref, band_ref, pw_ref, sc_ref, o_ref, *, seq, rows, unroll):
    nt = seq // rows
    h = half_ref[pl.program_id(1)]

    def window_sum(i):
        r0 = pl.multiple_of(i * rows, rows)
        rp = pl.multiple_of(jnp.maximum(i - 1, 0) * rows, rows)
        rn = pl.multiple_of(jnp.minimum(i + 1, nt - 1) * rows, rows)
        xc = u_ref[0, 0, pl.ds(r0, rows), :]
        s = jnp.dot(band_ref[0, 1], xc, preferred_element_type=F32)
        sp = jnp.dot(band_ref[0, 0], u_ref[0, 0, pl.ds(rp, rows), :], preferred_element_type=F32)
        sn = jnp.dot(band_ref[0, 2], u_ref[0, 0, pl.ds(rn, rows), :], preferred_element_type=F32)
        return r0, xc, s + jnp.where(i > 0, sp, 0.0) + jnp.where(i < nt - 1, sn, 0.0)

    def pooled(r0, xc, s):
        t = r0 + lax.broadcasted_iota(jnp.int32, (rows, LANE), 0)
        cnt = (jnp.minimum(t + h, seq) - jnp.maximum(t - h, 0)).astype(F32)
        return (s / cnt - xc.astype(F32)).astype(BF16)

    def tiles(j, carry):
        sums = [window_sum(j * unroll + k) for k in range(unroll)]
        pools = [pooled(*a) for a in sums]
        ys = [jnp.dot(pv, pw_ref[0], preferred_element_type=F32) * sc_ref[0] for pv in pools]
        for (r0, _, _), y in zip(sums, ys):
            o_ref[0, 0, pl.ds(r0, rows), :] = y.astype(BF16)
        return carry

    lax.fori_loop(0, nt // unroll, tiles, 0)


def _pool(u, pool_w, pool_scale):
    B, _, S, _ = u.shape
    G = pool_w.shape[0]
    rows = min(POOL_ROWS, S)
    band = jnp.asarray(_band_matrices(rows), BF16)
    halves = jnp.asarray([w // 2 for w in POOL_WINDOWS], jnp.int32)
    grid_spec = pltpu.PrefetchScalarGridSpec(
        num_scalar_prefetch=1,
        grid=(B, G),
        in_specs=[
            pl.BlockSpec((1, 1, S, LANE), lambda b, g, h: (b, g, 0, 0)),
            pl.BlockSpec((1, 3, rows, rows), lambda b, g, h: (g, 0, 0, 0)),
            pl.BlockSpec((1, LANE, LANE), lambda b, g, h: (g, 0, 0)),
            pl.BlockSpec((1, 1, LANE), lambda b, g, h: (g, 0, 0)),
        ],
        out_specs=pl.BlockSpec((1, 1, S, LANE), lambda b, g, h: (b, g, 0, 0)),
    )
    return pl.pallas_call(
        functools.partial(_pool_body, seq=S, rows=rows, unroll=POOL_UNROLL if (S // rows) % POOL_UNROLL == 0 else 1),
        grid_spec=grid_spec,
        out_shape=jax.ShapeDtypeStruct((B, G, S, LANE), BF16),
        compiler_params=_params(2),
        name="pool",
    )(halves, u, band, pool_w.astype(BF16), pool_scale.reshape(G, 1, LANE).astype(F32))


def _hg_gates(q_ref, f_ref, v_ref, r0, lb):
    rows = pl.ds(r0, HG_GROUP)
    q = q_ref[0, 0, rows, :].astype(F32)
    q = q * jax.nn.sigmoid(q)
    f = lb + (1.0 - lb) * jax.nn.sigmoid(f_ref[0, 0, rows, :].astype(F32))
    g = jnp.log(f)
    g_hi = g.astype(BF16)
    g_lo = (g - g_hi.astype(F32)).astype(BF16)
    return q, 1.0 - f, v_ref[0, 0, rows, :], g_hi, g_lo


def _chunk_rows(rows):
    return jnp.concatenate([jnp.broadcast_to(r, (HG_CHUNK, LANE)) for r in rows], axis=0)


def _hg_decays(q, k, b, *, reverse):
    C = HG_CHUNK
    chunks = range(HG_GROUP // C)
    end = [b[c * C + (0 if reverse else C - 1)][None] for c in chunks]
    mid = [b[c * C + (C // 2 if reverse else C // 2 - 1)][None] for c in chunks]
    b_mid = _chunk_rows(mid)
    q_mid = q * jnp.exp(b - b_mid)
    k_mid = k * jnp.exp(b_mid - b)
    q_dec = (q_mid * _chunk_rows([jnp.exp(m) for m in mid])).astype(BF16)
    k_st = (k_mid * _chunk_rows([jnp.exp(e - m) for e, m in zip(end, mid)])).astype(BF16)
    dec = [jnp.exp(e) for e in end]
    return q_dec, q_mid.astype(BF16), k_mid.astype(BF16), k_st, dec


def _hgrn_body(q_ref, ff_ref, fb_ref, v_ref, g_ref, lbf_ref, lbb_ref, gain_ref, tri_ref,
               o_ref, of_scr, ob_scr, *, seq, rows, unroll):
    C = HG_CHUNK
    R = HG_GROUP
    n_c = R // C
    n_groups = seq // R
    lb = (lbf_ref[0], lbb_ref[0])
    f_refs = (ff_ref, fb_ref)
    scr = (of_scr, ob_scr)
    ri = lax.broadcasted_iota(jnp.int32, (R, R), 0)
    ci = lax.broadcasted_iota(jnp.int32, (R, R), 1)
    same = (ri // C) == (ci // C)
    masks = (same & (ci <= ri), same & (ci >= ri))
    contract_last = (((1,), (1,)), ((), ()))
    contract_first = (((0,), (0,)), ((), ()))

    def step(j, carry):
        streams = []
        for d in range(2):
            for i in range(unroll):
                gi = j * unroll + i
                streams.append((d, pl.multiple_of((gi if d == 0 else n_groups - 1 - gi) * R, R)))
        gates = [_hg_gates(q_ref, f_refs[d], v_ref, r0, lb[d]) for d, r0 in streams]
        cums = [jnp.dot(tri_ref[d], g_hi, preferred_element_type=F32)
                + jnp.dot(tri_ref[d], g_lo, preferred_element_type=F32)
                for (d, _), (_, _, _, g_hi, g_lo) in zip(streams, gates)]
        ops = [_hg_decays(q, k, b, reverse=(d == 1)) for (d, _), (q, k, _, _, _), b in zip(streams, gates, cums)]
        scores = [lax.dot_general(q_in, k_in, contract_last, preferred_element_type=F32)
                  for _, q_in, k_in, _, _ in ops]
        kvs = [[lax.dot_general(v[c * C:(c + 1) * C], k_st[c * C:(c + 1) * C], contract_first,
                                preferred_element_type=F32) for c in range(n_c)]
               for (_, _, v, _, _), (_, _, _, k_st, _) in zip(gates, ops)]
        intra = [jnp.dot(jnp.where(masks[d], s, 0.0).astype(BF16), v, preferred_element_type=F32)
                 for (d, _), s, (_, _, v, _, _) in zip(streams, scores, gates)]
        states = list(carry)
        for n, (d, r0) in enumerate(streams):
            q_dec, dec = ops[n][0], ops[n][4]
            inter = [None] * n_c
            for c in (range(n_c) if d == 0 else reversed(range(n_c))):
                inter[c] = lax.dot_general(q_dec[c * C:(c + 1) * C], states[d].astype(BF16), contract_last,
                                           preferred_element_type=F32)
                states[d] = states[d] * dec[c] + kvs[n][c]
            scr[d][pl.ds(r0, R), :] = intra[n] + jnp.concatenate(inter, axis=0)
        return tuple(states)

    zero = jnp.zeros((LANE, LANE), F32)
    lax.fori_loop(0, n_groups // unroll, step, (zero, zero))

    def finish(i, carry):
        r = pl.ds(pl.multiple_of(i * rows, rows), rows)
        o = of_scr[r, :] + ob_scr[r, :]
        o = _rms(o, gain_ref[...])
        gate = g_ref[0, 0, r, :].astype(F32)
        o_ref[0, 0, r, :] = (o * (gate * jax.nn.sigmoid(gate))).astype(BF16)
        return carry

    lax.fori_loop(0, seq // rows, finish, 0)


def _hgrn(u, lb_f, lb_b, gain, n_pool):
    B, _, S, _ = u.shape
    H = lb_f.shape[0]
    C = HG_CHUNK
    R = HG_GROUP
    assert S % R == 0
    low = np.kron(np.eye(R // C), np.tril(np.ones((C, C)))).astype(np.float32)
    tri = jnp.asarray(np.stack([low, low.T]), BF16)
    rows = min(256, S)
    unroll = HG_UNROLL if (S // R) % HG_UNROLL == 0 else 1

    def slab(k):
        return pl.BlockSpec((1, 1, S, LANE), lambda b, h, k=k: (b, n_pool + k * H + h, 0, 0))

    head_vec = pl.BlockSpec((1, 1, LANE), lambda b, h: (h, 0, 0))
    return pl.pallas_call(
        functools.partial(_hgrn_body, seq=S, rows=rows, unroll=unroll),
        grid=(B, H),
        in_specs=[slab(0), slab(1), slab(2), slab(3), slab(4), head_vec, head_vec,
                  pl.BlockSpec((1, LANE), lambda b, h: (0, 0)),
                  pl.BlockSpec((2, R, R), lambda b, h: (0, 0, 0))],
        out_specs=pl.BlockSpec((1, 1, S, LANE), lambda b, h: (b, h, 0, 0)),
        out_shape=jax.ShapeDtypeStruct((B, H, S, LANE), BF16),
        scratch_shapes=[pltpu.VMEM((S, LANE), F32), pltpu.VMEM((S, LANE), F32)],
        compiler_params=_params(2),
        name="hgrn",
    )(u, u, u, u, u, lb_f, lb_b, gain, tri)


INFO_GATE = 0
INFO_LPOS = 2

TAB_COUNT = 0
TAB_BEFORE = 1
TAB_LSTART = 2

MOE_CHUNK = 8


def _outproj_body(mp_ref, mh_ref, x_ref, wo_ref, gain_ref, wrh_ref, wrl_ref, rb_ref, ls_ref, us_ref,
                  x2_ref, xn_ref, info_ref, tab_ref, before_ref, *, n_groups, epg):
    tm = x_ref.shape[1]

    @pl.when((pl.program_id(0) == 0) & (pl.program_id(1) == 0))
    def _():
        before_ref[...] = jnp.zeros_like(before_ref)

    mix = jnp.concatenate([mp_ref[0, j] for j in range(mp_ref.shape[1])]
                          + [mh_ref[0, j] for j in range(mh_ref.shape[1])], axis=-1)
    x2 = x_ref[0] + jnp.dot(mix, wo_ref[...], preferred_element_type=F32)
    x2_ref[0] = x2
    xn = _rms(x2, gain_ref[...])
    for j in range(xn.shape[1] // LANE):
        xn_ref[pl.ds(j, tm, stride=xn.shape[1] // LANE), :] = xn[:, j * LANE:(j + 1) * LANE]

    xh = xn.astype(BF16)
    xl = (xn - xh.astype(F32)).astype(BF16)
    logits = (jnp.dot(xh, wrh_ref[...], preferred_element_type=F32)
              + jnp.dot(xl, wrh_ref[...], preferred_element_type=F32)
              + jnp.dot(xh, wrl_ref[...], preferred_element_type=F32)) + rb_ref[...]

    lane = lax.broadcasted_iota(jnp.int32, (tm, LANE), 1)
    neg = jnp.float32(-jnp.inf)
    big = jnp.int32(LANE)

    def top(vals):
        m = jnp.max(vals, axis=-1, keepdims=True)
        return m, jnp.min(jnp.where(vals == m, lane, big), axis=-1, keepdims=True)

    glog = jnp.where(lane < n_groups, logits, neg)
    gmax, grp = top(glog)
    grp_prob = 1.0 / jnp.sum(jnp.exp(glog - gmax), axis=-1, keepdims=True)
    e_lo = n_groups + grp * epg
    elog = jnp.where((lane >= e_lo) & (lane < e_lo + epg), logits, neg)
    v1, i1 = top(elog)
    v2, i2 = top(jnp.where(lane == i1, neg, elog))
    e21 = jnp.exp(v2 - v1)
    gate1 = grp_prob / (1.0 + e21)
    gate2 = grp_prob * e21 / (1.0 + e21)

    hot1 = lane == i1
    hot2 = lane == i2
    onehot = jnp.where(hot1 | hot2, 1.0, 0.0)
    earlier = jnp.dot(ls_ref[...], onehot.astype(BF16), preferred_element_type=F32)
    chunks = jnp.ceil(jnp.sum(onehot, axis=0, keepdims=True) * (1.0 / MOE_CHUNK))
    chunks8 = jnp.broadcast_to(chunks, (SUBLANE, LANE))
    lstart = jnp.dot(chunks8.astype(BF16), us_ref[...], preferred_element_type=F32) * MOE_CHUNK
    pos = earlier + lstart[0:1]
    lpos1 = jnp.sum(jnp.where(hot1, pos, 0.0), axis=-1, keepdims=True)
    lpos2 = jnp.sum(jnp.where(hot2, pos, 0.0), axis=-1, keepdims=True)

    count = chunks8 * MOE_CHUNK
    row = lax.broadcasted_iota(jnp.int32, (SUBLANE, LANE), 0)
    tab_ref[...] = jnp.where(row == TAB_COUNT, count,
                             jnp.where(row == TAB_BEFORE, before_ref[...],
                                       jnp.where(row == TAB_LSTART, lstart, 0.0)))
    before_ref[...] = before_ref[...] + count

    info = jnp.zeros((tm, LANE), F32)
    for k, col in ((INFO_GATE, gate1), (INFO_GATE + 1, gate2), (INFO_LPOS, lpos1), (INFO_LPOS + 1, lpos2)):
        info = jnp.where(lane == k, col, info)
    info_ref[0] = info


def _outproj(mixp, mixh, x, w_out, gain, wr_hi, wr_lo, rbias, tm, n_groups, epg):
    B, S, D = x.shape
    n_sl = D // LANE
    lstrict = jnp.asarray(np.tril(np.ones((tm, tm), np.float32), -1), BF16)
    ustrict = jnp.asarray(np.triu(np.ones((LANE, LANE), np.float32), 1), BF16)
    const = lambda b, i: (0, 0)
    tile = lambda b, i: (b * (S // tm) + i, 0)
    return pl.pallas_call(
        functools.partial(_outproj_body, n_groups=n_groups, epg=epg),
        grid=(B, S // tm),
        in_specs=[
            pl.BlockSpec((1, mixp.shape[1], tm, LANE), lambda b, i: (b, 0, i, 0)),
            pl.BlockSpec((1, mixh.shape[1], tm, LANE), lambda b, i: (b, 0, i, 0)),
            pl.BlockSpec((1, tm, D), lambda b, i: (b, i, 0)),
            pl.BlockSpec(w_out.shape, const),
            pl.BlockSpec((1, D), const),
            pl.BlockSpec((D, LANE), const),
            pl.BlockSpec((D, LANE), const),
            pl.BlockSpec((1, LANE), const),
            pl.BlockSpec((tm, tm), const),
            pl.BlockSpec((LANE, LANE), const),
        ],
        out_specs=[
            pl.BlockSpec((1, tm, D), lambda b, i: (b, i, 0)),
            pl.BlockSpec((tm * n_sl, LANE), tile),
            pl.BlockSpec((1, tm, LANE), lambda b, i: (b, i, 0)),
            pl.BlockSpec((SUBLANE, LANE), tile),
        ],
        out_shape=[
            jax.ShapeDtypeStruct((B, S, D), F32),
            jax.ShapeDtypeStruct((B * S * n_sl, LANE), F32),
            jax.ShapeDtypeStruct((B, S, LANE), F32),
            jax.ShapeDtypeStruct((B * (S // tm) * SUBLANE, LANE), F32),
        ],
        scratch_shapes=[pltpu.VMEM((SUBLANE, LANE), F32)],
        compiler_params=_params(2),
        name="outproj",
    )(mixp, mixh, x, w_out, gain, wr_hi, wr_lo, rbias, lstrict, ustrict)


def _dispatch_body(nloc_ref, pad_ref, p1_ref, p2_ref, dst_ref, xn_ref, xs_ref, buf_a, buf_b, zbuf, sem,
                   *, tm, n_sl, n_ch, n_fix):
    i = pl.program_id(0)
    last = pl.num_programs(0) - 1
    piece = MOE_CHUNK * n_sl
    stride = _index_stride(n_ch)

    def chunk_copy(buf, j, chunk, s):
        d = pl.multiple_of(chunk * piece, piece)
        return pltpu.make_async_copy(buf.at[pl.ds(j * piece, piece)], xs_ref.at[pl.ds(d, piece)], sem.at[s])

    def start(buf, half, s):
        n_loc = nloc_ref[2 * i + half]
        for j in range(n_ch):
            copy = chunk_copy(buf, j, dst_ref[half * stride + j], s)
            if j < n_fix:
                copy.start()
            else:
                pl.when(j < n_loc)(copy.start)

    def wait_chunks(buf, n, s):
        pltpu.make_async_copy(buf.at[pl.ds(0, n * piece)], xs_ref.at[pl.ds(0, n * piece)], sem.at[s]).wait()

    def wait(buf, tile, s):
        n_loc = nloc_ref[tile]
        wait_chunks(buf, n_fix, s)
        for j in range(n_fix, n_ch):
            pl.when(j < n_loc)(functools.partial(wait_chunks, buf, 1, s))

    def sort(buf, half):
        buf[...] = jnp.zeros_like(buf)

        def move(t, carry):
            row = xn_ref[pl.ds(pl.multiple_of((half * tm + t) * n_sl, n_sl), n_sl), :]
            buf[pl.ds(pl.multiple_of(p1_ref[half * tm + t] * n_sl, n_sl), n_sl), :] = row
            buf[pl.ds(pl.multiple_of(p2_ref[half * tm + t] * n_sl, n_sl), n_sl), :] = row
            return carry

        lax.fori_loop(0, tm, move, 0, unroll=8)

    @pl.when(i == 0)
    def _():
        zbuf[...] = jnp.zeros_like(zbuf)
        n_listed = pad_ref.shape[0] - 1
        n_chunks = xs_ref.shape[0] // piece

        def fill(k, carry):
            pl.when(pad_ref[k] >= 0)(chunk_copy(zbuf, 0, jnp.maximum(pad_ref[k], 0), 2).start)
            return carry

        def filled(k, carry):
            pl.when(pad_ref[k] >= 0)(functools.partial(wait_chunks, zbuf, 1, 2))
            return carry

        def fill_tail(c, carry):
            chunk_copy(zbuf, 0, c, 2).start()
            return carry

        def filled_tail(c, carry):
            wait_chunks(zbuf, 1, 2)
            return carry

        lax.fori_loop(0, n_listed, fill, 0)
        lax.fori_loop(pad_ref[n_listed], n_chunks, fill_tail, 0)
        lax.fori_loop(0, n_listed, filled, 0)
        lax.fori_loop(pad_ref[n_listed], n_chunks, filled_tail, 0)

    @pl.when(i > 0)
    def _():
        wait(buf_a, 2 * i - 2, 0)

    sort(buf_a, 0)
    start(buf_a, 0, 0)

    @pl.when(i > 0)
    def _():
        wait(buf_b, 2 * i - 1, 1)

    sort(buf_b, 1)
    start(buf_b, 1, 1)

    @pl.when(i == last)
    def _():
        wait(buf_a, 2 * i, 0)
        wait(buf_b, 2 * i + 1, 1)


def _dispatch(n_local, pad_chunks, lpos1, lpos2, chunk_dst, xn_rows, n_slot_chunks, tm, cap, n_sl):
    n_tiles = lpos1.shape[0] // tm
    assert n_tiles % 2 == 0
    n_ch = cap // MOE_CHUNK
    stride = _index_stride(n_ch)
    piece = MOE_CHUNK * n_sl
    grid_spec = pltpu.PrefetchScalarGridSpec(
        num_scalar_prefetch=2,
        grid=(n_tiles // 2,),
        in_specs=[
            pl.BlockSpec((2 * tm,), lambda i, nl, pc: (i,), memory_space=pltpu.SMEM),
            pl.BlockSpec((2 * tm,), lambda i, nl, pc: (i,), memory_space=pltpu.SMEM),
            pl.BlockSpec((2 * stride,), lambda i, nl, pc: (i,), memory_space=pltpu.SMEM),
            pl.BlockSpec((2 * tm * n_sl, LANE), lambda i, nl, pc: (i, 0)),
        ],
        out_specs=pl.BlockSpec(memory_space=pl.ANY),
        scratch_shapes=[pltpu.VMEM((cap * n_sl, LANE), F32), pltpu.VMEM((cap * n_sl, LANE), F32),
                        pltpu.VMEM((piece, LANE), F32), pltpu.SemaphoreType.DMA((3,))],
    )
    return pl.pallas_call(
        functools.partial(_dispatch_body, tm=tm, n_sl=n_sl, n_ch=n_ch, n_fix=TOP_K * tm // MOE_CHUNK),
        grid_spec=grid_spec,
        out_shape=jax.ShapeDtypeStruct((n_slot_chunks * piece, LANE), F32),
        compiler_params=_params(1),
        name="dispatch",
    )(n_local, pad_chunks, lpos1, lpos2, _pad_lists(chunk_dst, n_ch), xn_rows)


def _start_rows(idx_ref, first, n, src_ref, dst_ref, sem, rows_per):
    for r in range(n):
        s = pl.multiple_of(idx_ref[first + r] * rows_per, rows_per)
        pltpu.make_async_copy(src_ref.at[pl.ds(s, rows_per)], dst_ref.at[pl.ds(r * rows_per, rows_per)], sem).start()


def _wait_rows(n, src_ref, dst_ref, sem, rows_per):
    total = n * rows_per
    pltpu.make_async_copy(src_ref.at[pl.ds(0, total)], dst_ref.at[pl.ds(0, total)], sem).wait()


def _index_stride(n):
    return max(LANE, pl.next_power_of_2(n))


def _pad_lists(idx, n):
    return jnp.pad(idx.reshape(-1, n), ((0, 0), (0, _index_stride(n) - n))).reshape(-1)


def _slab_rows(ref, n, n_sl, first=0):
    return jnp.concatenate([ref[pl.ds(first + j, n, stride=n_sl), :] for j in range(n_sl)], axis=-1)


def _experts_body(bexp_ref, nused_ref, xs_ref, w1a_ref, w3a_ref, w2a_ref, w1b_ref, w3b_ref, w2b_ref, y_ref, *, n_sl):
    i = pl.program_id(0)
    blk = EXPERT_ROWS

    def mlp(w1_ref, w3_ref, w2_ref, half):
        first = half * blk * n_sl
        xb = _slab_rows(xs_ref, blk, n_sl, first).astype(BF16)
        a = jnp.dot(xb, w1_ref[0], preferred_element_type=F32)
        h = (a * jax.nn.sigmoid(a)) * jnp.dot(xb, w3_ref[0], preferred_element_type=F32)
        y = jnp.dot(h.astype(BF16), w2_ref[0], preferred_element_type=F32)
        for j in range(n_sl):
            y_ref[pl.ds(first + j, blk, stride=n_sl), :] = y[:, j * LANE:(j + 1) * LANE]

    @pl.when(2 * i < nused_ref[0])
    def _():
        mlp(w1a_ref, w3a_ref, w2a_ref, 0)
        mlp(w1b_ref, w3b_ref, w2b_ref, 1)

    @pl.when(2 * i >= nused_ref[0])
    def _():
        y_ref[...] = jnp.zeros_like(y_ref)


def _experts(block_expert, n_used, xs_rows, w1, w3, w2, n_sl):
    n_blocks = block_expert.shape[0]
    assert n_blocks % 2 == 0
    blk = EXPERT_ROWS
    _, D, DE = w1.shape

    def wspec(shape, half):
        return pl.BlockSpec(shape, lambda i, be, nu, half=half: (be[2 * i + half], 0, 0))

    grid_spec = pltpu.PrefetchScalarGridSpec(
        num_scalar_prefetch=2,
        grid=(n_blocks // 2,),
        in_specs=[
            pl.BlockSpec((2 * blk * n_sl, LANE), lambda i, be, nu: (jnp.minimum(i, nu[0] // 2 - 1), 0)),
            wspec((1, D, DE), 0), wspec((1, D, DE), 0), wspec((1, DE, D), 0),
            wspec((1, D, DE), 1), wspec((1, D, DE), 1), wspec((1, DE, D), 1),
        ],
        out_specs=pl.BlockSpec((2 * blk * n_sl, LANE), lambda i, be, nu: (i, 0)),
    )
    return pl.pallas_call(
        functools.partial(_experts_body, n_sl=n_sl),
        grid_spec=grid_spec,
        out_shape=jax.ShapeDtypeStruct((n_blocks * blk * n_sl, LANE), F32),
        compiler_params=_params(1),
        name="experts",
    )(block_expert, n_used, xs_rows, w1, w3, w2, w1, w3, w2)


def _final_body(p1_ref, p2_ref, src_ref, nxt_ref, x2_ref, info_ref, gain_ref, y_ref, o_ref,
                ybuf_a, ybuf_b, tok1, tok2, sem, *, n_sl, tm, n_ch):
    i = pl.program_id(0)
    piece = MOE_CHUNK * n_sl

    def combine(ybuf, half):
        def unsort(t, carry):
            dst = pl.ds(pl.multiple_of(t * n_sl, n_sl), n_sl)
            tok1[dst, :] = ybuf[pl.ds(pl.multiple_of(p1_ref[half * tm + t] * n_sl, n_sl), n_sl), :]
            tok2[dst, :] = ybuf[pl.ds(pl.multiple_of(p2_ref[half * tm + t] * n_sl, n_sl), n_sl), :]
            return carry

        lax.fori_loop(0, tm, unsort, 0, unroll=8)
        rows = pl.ds(half * tm, tm)
        info = info_ref[rows, :]
        moe = (_slab_rows(tok1, tm, n_sl) * info[:, INFO_GATE:INFO_GATE + 1]
               + _slab_rows(tok2, tm, n_sl) * info[:, INFO_GATE + 1:INFO_GATE + 2])
        o_ref[rows, :] = _rms(x2_ref[rows, :] + moe, gain_ref[...])

    @pl.when(i == 0)
    def _():
        _start_rows(src_ref, 0, n_ch, y_ref, ybuf_a, sem.at[0], piece)

    _start_rows(src_ref, _index_stride(n_ch), n_ch, y_ref, ybuf_b, sem.at[1], piece)
    _wait_rows(n_ch, y_ref, ybuf_a, sem.at[0], piece)
    combine(ybuf_a, 0)
    _start_rows(nxt_ref, 0, n_ch, y_ref, ybuf_a, sem.at[0], piece)
    _wait_rows(n_ch, y_ref, ybuf_b, sem.at[1], piece)
    combine(ybuf_b, 1)

    @pl.when(i == pl.num_programs(0) - 1)
    def _():
        _wait_rows(n_ch, y_ref, ybuf_a, sem.at[0], piece)


def _final(lpos1, lpos2, y_src, x2, info, gain, y_rows, tm, cap, n_sl):
    T, D = x2.shape
    assert T % (2 * tm) == 0
    steps = T // (2 * tm)
    n_ch = cap // MOE_CHUNK
    stride = _index_stride(n_ch)
    y_src = _pad_lists(y_src, n_ch)
    cur = lambda i: (i,)
    nxt = lambda i: (jnp.minimum(2 * i + 2, 2 * steps - 1),)
    ybuf = pltpu.VMEM((cap * n_sl, LANE), F32)
    tbuf = pltpu.VMEM((tm * n_sl, LANE), F32)
    return pl.pallas_call(
        functools.partial(_final_body, n_sl=n_sl, tm=tm, n_ch=n_ch),
        grid=(steps,),
        in_specs=[
            pl.BlockSpec((2 * tm,), cur, memory_space=pltpu.SMEM),
            pl.BlockSpec((2 * tm,), cur, memory_space=pltpu.SMEM),
            pl.BlockSpec((2 * stride,), cur, memory_space=pltpu.SMEM),
            pl.BlockSpec((stride,), nxt, memory_space=pltpu.SMEM),
            pl.BlockSpec((2 * tm, D), lambda i: (i, 0)),
            pl.BlockSpec((2 * tm, LANE), lambda i: (i, 0)),
            pl.BlockSpec((1, D), lambda i: (0, 0)),
            pl.BlockSpec(memory_space=pl.ANY),
        ],
        out_specs=pl.BlockSpec((2 * tm, D), lambda i: (i, 0)),
        out_shape=jax.ShapeDtypeStruct((T, D), F32),
        scratch_shapes=[ybuf, ybuf, tbuf, tbuf, pltpu.SemaphoreType.DMA((2,))],
        compiler_params=_params(1),
        name="final",
    )(lpos1, lpos2, y_src, y_src, x2, info, gain, y_rows)


def _tile(n, pref):
    return pref if n % pref == 0 else n


def _layer(x, p):
    B, S, D = x.shape
    T = B * S
    n_sl = D // LANE
    G = p["pool_w"].shape[0]
    E = p["w1"].shape[0]
    n_groups = p["n_groups"]
    tm = _tile(S, 512)
    n_tiles = T // tm

    u = _inproj(x, p["norm_mix"], p["w_in"], tm)
    mixp = _pool(u, p["pool_w"], p["pool_scale"])
    mixh = _hgrn(u, p["lb_f"], p["lb_b"], p["hg_gain"], G)
    x2, xn_rows, info, tab = _outproj(mixp, mixh, x, p["w_out"], p["norm_ffn"], p["wr_hi"], p["wr_lo"],
                                      p["rbias"], tm, n_groups, E // n_groups)

    blk = EXPERT_ROWS
    ch = MOE_CHUNK
    cap = TOP_K * tm + E * ch
    cap_ch = cap // ch
    n_blocks = -(-(T * TOP_K + n_tiles * E * (ch - 1)) // blk) + E + 1
    n_blocks += n_blocks % 2
    n_slot_chunks = n_blocks * blk // ch
    tab = tab.reshape(n_tiles, SUBLANE, LANE)[:, :, n_groups:n_groups + E].astype(jnp.int32)
    count, before, lstart = tab[:, TAB_COUNT], tab[:, TAB_BEFORE], tab[:, TAB_LSTART]
    total = before[-1] + count[-1]
    padded = (total + blk - 1) // blk * blk
    padded = padded.at[E - 1].add(blk * ((jnp.sum(padded) // blk) % 2))
    pend = jnp.cumsum(padded)
    pstart = pend - padded
    first_slot = jnp.arange(n_blocks, dtype=jnp.int32) * blk
    block_expert = jnp.minimum(jnp.sum(pend[None, :] <= first_slot[:, None], axis=1), E - 1).astype(jnp.int32)
    n_used = (pend[-1:] // blk).astype(jnp.int32)

    slot_chunk = (pstart[None, :] + before) // ch
    local_chunk = lstart // ch
    n_local = (lstart[:, -1] + count[:, -1]) // ch
    off = slot_chunk - local_chunk
    step = off - jnp.concatenate([jnp.zeros((n_tiles, 1), jnp.int32), off[:, :-1]], axis=1)
    steps = jnp.zeros((n_tiles, cap_ch + 1), jnp.int32).at[jnp.arange(n_tiles)[:, None], local_chunk].add(step)
    chunk_slot = jnp.arange(cap_ch, dtype=jnp.int32)[None, :] + jnp.cumsum(steps, axis=1)[:, :cap_ch]
    chunk_slot = jnp.clip(chunk_slot, 0, n_slot_chunks - 1).reshape(-1)
    k = jnp.arange(2 * blk // ch, dtype=jnp.int32)[None, :]
    pad_chunks = ((pstart + total) // ch)[:, None] + k
    pad_chunks = jnp.where(pad_chunks < (pend // ch)[:, None], pad_chunks, -1).reshape(-1)
    pad_chunks = jnp.concatenate([pad_chunks, pend[-1:] // ch])

    info2 = info.reshape(T, LANE)
    lpos = info2[:, INFO_LPOS:INFO_LPOS + TOP_K].astype(jnp.int32)
    xs_rows = _dispatch(n_local, pad_chunks, lpos[:, 0], lpos[:, 1], chunk_slot, xn_rows, n_slot_chunks, tm, cap, n_sl)
    y_rows = _experts(block_expert, n_used, xs_rows, p["w1"], p["w3"], p["w2"], n_sl)
    out = _final(lpos[:, 0], lpos[:, 1], chunk_slot, x2.reshape(T, D), info2, p["norm_final"], y_rows, tm, cap, n_sl)
    return out.reshape(B, S, D)


def kernel(x_prompt, x_sample, w_in, w_out, pool_w, pool_scale, hg_lb_fwd, hg_lb_bwd, hg_norm_gain, norm_mix, norm_ffn, router_group_w, router_group_b, router_expert_w, router_expert_b, expert_w1, expert_w3, expert_w2, norm_final):
    depth = w_in.shape[0]
    assert depth == 1, "the final norm is fused into the last layer's combine kernel; one layer supported"
    D = w_in.shape[1]
    hg_width = hg_lb_fwd.shape[1]
    dv = hg_norm_gain.shape[1]
    pg = pool_w.shape[2]
    assert dv == LANE and pg == LANE and D % LANE == 0
    H = hg_width // dv
    n_groups = router_group_w.shape[-1]
    E = router_expert_w.shape[-1]
    assert n_groups + E <= LANE

    lb_f = jnp.cumsum(jax.nn.softmax(hg_lb_fwd.astype(F32), axis=0), axis=0)
    lb_b = jnp.cumsum(jax.nn.softmax(hg_lb_bwd.astype(F32), axis=0), axis=0)
    l = 0
    wr = jnp.concatenate([router_group_w[l], router_expert_w[l]], axis=1).astype(F32)
    wr = jnp.pad(wr, ((0, 0), (0, LANE - wr.shape[1])))
    wr_hi = wr.astype(BF16)
    rbias = jnp.concatenate([router_group_b[l], router_expert_b[l]]).astype(F32)
    p = dict(
        n_groups=n_groups,
        w_in=w_in[l].astype(BF16), w_out=w_out[l].astype(BF16),
        pool_w=pool_w[l], pool_scale=pool_scale[l],
        lb_f=lb_f[l].reshape(H, 1, dv), lb_b=lb_b[l].reshape(H, 1, dv), hg_gain=hg_norm_gain[l].reshape(1, dv).astype(F32),
        norm_mix=norm_mix[l].reshape(1, D).astype(F32), norm_ffn=norm_ffn[l].reshape(1, D).astype(F32),
        norm_final=norm_final.reshape(1, D).astype(F32),
        wr_hi=wr_hi, wr_lo=(wr - wr_hi.astype(F32)).astype(BF16),
        rbias=jnp.pad(rbias, (0, LANE - rbias.shape[0])).reshape(1, LANE),
        w1=expert_w1[l].astype(BF16), w3=expert_w3[l].astype(BF16), w2=expert_w2[l].astype(BF16),
    )
    return (_layer(x_prompt, p), _layer(x_sample, p))
```

```python
import functools

import jax
import jax.numpy as jnp
import numpy as np
from jax import lax
from jax.experimental import pallas as pl
from jax.experimental.pallas import tpu as pltpu

F32 = jnp.float32
BF16 = jnp.bfloat16

EPS = 1e-6
POOL_WINDOWS = (2, 4, 8, 16)
TOP_K = 2

LANE = 128
SUBLANE = 8
VMEM_LIMIT = 56 * 1024 * 1024

HG_CHUNK = 32
HG_GROUP = 128
HG_UNROLL = 4
POOL_ROWS = 128
POOL_UNROLL = 4
EXPERT_ROWS = 256


def _params(n_axes):
    return pltpu.CompilerParams(dimension_semantics=("arbitrary",) * n_axes, vmem_limit_bytes=VMEM_LIMIT)


def _rms(x, gain):
    return x * lax.rsqrt(jnp.mean(x * x, axis=-1, keepdims=True) + EPS) * gain


def _inproj_body(x_ref, gain_ref, w_ref, u_ref, *, nc):
    n = _rms(x_ref[0], gain_ref[...]).astype(BF16)
    per = nc // LANE
    for c in range(w_ref.shape[1] // nc):
        r = jnp.dot(n, w_ref[:, c * nc:(c + 1) * nc], preferred_element_type=F32)
        for j in range(per):
            u_ref[0, c * per + j] = r[:, j * LANE:(j + 1) * LANE].astype(BF16)


def _inproj(x, gain, w, tm):
    B, S, D = x.shape
    cols = w.shape[1]
    return pl.pallas_call(
        functools.partial(_inproj_body, nc=4 * LANE),
        grid=(B, S // tm),
        in_specs=[
            pl.BlockSpec((1, tm, D), lambda b, i: (b, i, 0)),
            pl.BlockSpec((1, D), lambda b, i: (0, 0)),
            pl.BlockSpec((D, cols), lambda b, i: (0, 0)),
        ],
        out_specs=pl.BlockSpec((1, cols // LANE, tm, LANE), lambda b, i: (b, 0, i, 0)),
        out_shape=jax.ShapeDtypeStruct((B, cols // LANE, S, LANE), BF16),
        compiler_params=_params(2),
        name="inproj",
    )(x, gain, w)


def _band_matrices(rows):
    t = np.arange(rows)[:, None]
    s = np.arange(rows)[None, :]
    out = np.zeros((len(POOL_WINDOWS), 3, rows, rows), np.float32)
    for gi, w in enumerate(POOL_WINDOWS):
        for k, shift in enumerate((-rows, 0, rows)):
            pos = s + shift
            out[gi, k] = (pos >= t - w // 2) & (pos < t + w // 2)
    return out


def _pool_body(half_ref, u_ref, band_ref, pw_ref, sc_ref, o_ref, *, seq, rows, unroll):
    nt = seq // rows
    h = half_ref[pl.program_id(1)]

    def window_sum(i):
        r0 = pl.multiple_of(i * rows, rows)
        rp = pl.multiple_of(jnp.maximum(i - 1, 0) * rows, rows)
        rn = pl.multiple_of(jnp.minimum(i + 1, nt - 1) * rows, rows)
        xc = u_ref[0, 0, pl.ds(r0, rows), :]
        s = jnp.dot(band_ref[0, 1], xc, preferred_element_type=F32)
        sp = jnp.dot(band_ref[0, 0], u_ref[0, 0, pl.ds(rp, rows), :], preferred_element_type=F32)
        sn = jnp.dot(band_ref[0, 2], u_ref[0, 0, pl.ds(rn, rows), :], preferred_element_type=F32)
        return r0, xc, s + jnp.where(i > 0, sp, 0.0) + jnp.where(i < nt - 1, sn, 0.0)

    def pooled(r0, xc, s):
        t = r0 + lax.broadcasted_iota(jnp.int32, (rows, LANE), 0)
        cnt = (jnp.minimum(t + h, seq) - jnp.maximum(t - h, 0)).astype(F32)
        return (s / cnt - xc.astype(F32)).astype(BF16)

    def tiles(j, carry):
        sums = [window_sum(j * unroll + k) for k in range(unroll)]
        pools = [pooled(*a) for a in sums]
        ys = [jnp.dot(pv, pw_ref[0], preferred_element_type=F32) * sc_ref[0] for pv in pools]
        for (r0, _, _), y in zip(sums, ys):
            o_ref[0, 0, pl.ds(r0, rows), :] = y.astype(BF16)
        return carry

    lax.fori_loop(0, nt // unroll, tiles, 0)


def _pool(u, pool_w, pool_scale):
    B, _, S, _ = u.shape
    G = pool_w.shape[0]
    rows = min(POOL_ROWS, S)
    band = jnp.asarray(_band_matrices(rows), BF16)
    halves = jnp.asarray([w // 2 for w in POOL_WINDOWS], jnp.int32)
    grid_spec = pltpu.PrefetchScalarGridSpec(
        num_scalar_prefetch=1,
        grid=(B, G),
        in_specs=[
            pl.BlockSpec((1, 1, S, LANE), lambda b, g, h: (b, g, 0, 0)),
            pl.BlockSpec((1, 3, rows, rows), lambda b, g, h: (g, 0, 0, 0)),
            pl.BlockSpec((1, LANE, LANE), lambda b, g, h: (g, 0, 0)),
            pl.BlockSpec((1, 1, LANE), lambda b, g, h: (g, 0, 0)),
        ],
        out_specs=pl.BlockSpec((1, 1, S, LANE), lambda b, g, h: (b, g, 0, 0)),
    )
    return pl.pallas_call(
        functools.partial(_pool_body, seq=S, rows=rows, unroll=POOL_UNROLL if (S // rows) % POOL_UNROLL == 0 else 1),
        grid_spec=grid_spec,
        out_shape=jax.ShapeDtypeStruct((B, G, S, LANE), BF16),
        compiler_params=_params(2),
        name="pool",
    )(halves, u, band, pool_w.astype(BF16), pool_scale.reshape(G, 1, LANE).astype(F32))


def _hg_gates(q_ref, f_ref, v_ref, r0, lb):
    rows = pl.ds(r0, HG_GROUP)
    q = q_ref[0, 0, rows, :].astype(F32)
    q = q * jax.nn.sigmoid(q)
    f = lb + (1.0 - lb) * jax.nn.sigmoid(f_ref[0, 0, rows, :].astype(F32))
    g = jnp.log(f)
    g_hi = g.astype(BF16)
    g_lo = (g - g_hi.astype(F32)).astype(BF16)
    return q, 1.0 - f, v_ref[0, 0, rows, :], g_hi, g_lo


def _chunk_rows(rows):
    return jnp.concatenate([jnp.broadcast_to(r, (HG_CHUNK, LANE)) for r in rows], axis=0)


def _hg_decays(q, k, b, *, reverse):
    C = HG_CHUNK
    chunks = range(HG_GROUP // C)
    end = [b[c * C + (0 if reverse else C - 1)][None] for c in chunks]
    mid = [b[c * C + (C // 2 if reverse else C // 2 - 1)][None] for c in chunks]
    b_mid = _chunk_rows(mid)
    q_mid = q * jnp.exp(b - b_mid)
    k_mid = k * jnp.exp(b_mid - b)
    q_dec = (q_mid * _chunk_rows([jnp.exp(m) for m in mid])).astype(BF16)
    k_st = (k_mid * _chunk_rows([jnp.exp(e - m) for e, m in zip(end, mid)])).astype(BF16)
    dec = [jnp.exp(e) for e in end]
    return q_dec, q_mid.astype(BF16), k_mid.astype(BF16), k_st, dec


def _hgrn_body(q_ref, ff_ref, fb_ref, v_ref, g_ref, lbf_ref, lbb_ref, gain_ref, tri_ref,
               o_ref, of_scr, ob_scr, *, seq, rows, unroll):
    C = HG_CHUNK
    R = HG_GROUP
    n_c = R // C
    n_groups = seq // R
    lb = (lbf_ref[0], lbb_ref[0])
    f_refs = (ff_ref, fb_ref)
    scr = (of_scr, ob_scr)
    ri = lax.broadcasted_iota(jnp.int32, (R, R), 0)
    ci = lax.broadcasted_iota(jnp.int32, (R, R), 1)
    same = (ri // C) == (ci // C)
    masks = (same & (ci <= ri), same & (ci >= ri))
    contract_last = (((1,), (1,)), ((), ()))
    contract_first = (((0,), (0,)), ((), ()))

    def step(j, carry):
        streams = []
        for d in range(2):
            for i in range(unroll):
                gi = j * unroll + i
                streams.append((d, pl.multiple_of((gi if d == 0 else n_groups - 1 - gi) * R, R)))
        gates = [_hg_gates(q_ref, f_refs[d], v_ref, r0, lb[d]) for d, r0 in streams]
        cums = [jnp.dot(tri_ref[d], g_hi, preferred_element_type=F32)
                + jnp.dot(tri_ref[d], g_lo, preferred_element_type=F32)
                for (d, _), (_, _, _, g_hi, g_lo) in zip(streams, gates)]
        ops = [_hg_decays(q, k, b, reverse=(d == 1)) for (d, _), (q, k, _, _, _), b in zip(streams, gates, cums)]
        scores = [lax.dot_general(q_in, k_in, contract_last, preferred_element_type=F32)
                  for _, q_in, k_in, _, _ in ops]
        kvs = [[lax.dot_general(v[c * C:(c + 1) * C], k_st[c * C:(c + 1) * C], contract_first,
                                preferred_element_type=F32) for c in range(n_c)]
               for (_, _, v, _, _), (_, _, _, k_st, _) in zip(gates, ops)]
        intra = [jnp.dot(jnp.where(masks[d], s, 0.0).astype(BF16), v, preferred_element_type=F32)
                 for (d, _), s, (_, _, v, _, _) in zip(streams, scores, gates)]
        states = list(carry)
        for n, (d, r0) in enumerate(streams):
            q_dec, dec = ops[n][0], ops[n][4]
            inter = [None] * n_c
            for c in (range(n_c) if d == 0 else reversed(range(n_c))):
                inter[c] = lax.dot_general(q_dec[c * C:(c + 1) * C], states[d].astype(BF16), contract_last,
                                           preferred_element_type=F32)
                states[d] = states[d] * dec[c] + kvs[n][c]
            scr[d][pl.ds(r0, R), :] = intra[n] + jnp.concatenate(inter, axis=0)
        return tuple(states)

    zero = jnp.zeros((LANE, LANE), F32)
    lax.fori_loop(0, n_groups // unroll, step, (zero, zero))

    def finish(i, carry):
        r = pl.ds(pl.multiple_of(i * rows, rows), rows)
        o = of_scr[r, :] + ob_scr[r, :]
        o = _rms(o, gain_ref[...])
        gate = g_ref[0, 0, r, :].astype(F32)
        o_ref[0, 0, r, :] = (o * (gate * jax.nn.sigmoid(gate))).astype(BF16)
        return carry

    lax.fori_loop(0, seq // rows, finish, 0)


def _hgrn(u, lb_f, lb_b, gain, n_pool):
    B, _, S, _ = u.shape
    H = lb_f.shape[0]
    C = HG_CHUNK
    R = HG_GROUP
    assert S % R == 0
    low = np.kron(np.eye(R // C), np.tril(np.ones((C, C)))).astype(np.float32)
    tri = jnp.asarray(np.stack([low, low.T]), BF16)
    rows = min(256, S)
    unroll = HG_UNROLL if (S // R) % HG_UNROLL == 0 else 1

    def slab(k):
        return pl.BlockSpec((1, 1, S, LANE), lambda b, h, k=k: (b, n_pool + k * H + h, 0, 0))

    head_vec = pl.BlockSpec((1, 1, LANE), lambda b, h: (h, 0, 0))
    return pl.pallas_call(
        functools.partial(_hgrn_body, seq=S, rows=rows, unroll=unroll),
        grid=(B, H),
        in_specs=[slab(0), slab(1), slab(2), slab(3), slab(4), head_vec, head_vec,
                  pl.BlockSpec((1, LANE), lambda b, h: (0, 0)),
                  pl.BlockSpec((2, R, R), lambda b, h: (0, 0, 0))],
        out_specs=pl.BlockSpec((1, 1, S, LANE), lambda b, h: (b, h, 0, 0)),
        out_shape=jax.ShapeDtypeStruct((B, H, S, LANE), BF16),
        scratch_shapes=[pltpu.VMEM((S, LANE), F32), pltpu.VMEM((S, LANE), F32)],
        compiler_params=_params(2),
        name="hgrn",
    )(u, u, u, u, u, lb_f, lb_b, gain, tri)


INFO_GATE = 0
INFO_LPOS = 2

TAB_COUNT = 0
TAB_BEFORE = 1
TAB_LSTART = 2

MOE_CHUNK = 8


def _outproj_body(mp_ref, mh_ref, x_ref, wo_ref, gain_ref, wr_ref, rb_ref, ls_ref, us_ref,
                  x2_ref, xn_ref, info_ref, tab_ref, before_ref, *, n_groups, epg):
    tm = x_ref.shape[1]

    @pl.when((pl.program_id(0) == 0) & (pl.program_id(1) == 0))
    def _():
        before_ref[...] = jnp.zeros_like(before_ref)

    mix = jnp.concatenate([mp_ref[0, j] for j in range(mp_ref.shape[1])]
                          + [mh_ref[0, j] for j in range(mh_ref.shape[1])], axis=-1)
    x2 = x_ref[0] + jnp.dot(mix, wo_ref[...], preferred_element_type=F32)
    x2_ref[0] = x2
    xn = _rms(x2, gain_ref[...])
    for j in range(xn.shape[1] // LANE):
        xn_ref[pl.ds(j, tm, stride=xn.shape[1] // LANE), :] = xn[:, j * LANE:(j + 1) * LANE]

    xh = xn.astype(BF16)
    xl = (xn - xh.astype(F32)).astype(BF16)
    hh_hl = jnp.dot(xh, wr_ref[...], preferred_element_type=F32)
    logits = (hh_hl[:, :LANE] + hh_hl[:, LANE:]
              + jnp.dot(xl, wr_ref[:, :LANE], preferred_element_type=F32)) + rb_ref[...]

    lane = lax.broadcasted_iota(jnp.int32, (tm, LANE), 1)
    neg = jnp.float32(-jnp.inf)
    big = jnp.int32(LANE)

    def top(vals):
        m = jnp.max(vals, axis=-1, keepdims=True)
        return m, jnp.min(jnp.where(vals == m, lane, big), axis=-1, keepdims=True)

    glog = jnp.where(lane < n_groups, logits, neg)
    gmax, grp = top(glog)
    grp_prob = 1.0 / jnp.sum(jnp.exp(glog - gmax), axis=-1, keepdims=True)
    e_lo = n_groups + grp * epg
    elog = jnp.where((lane >= e_lo) & (lane < e_lo + epg), logits, neg)
    v1, i1 = top(elog)
    v2, i2 = top(jnp.where(lane == i1, neg, elog))
    e21 = jnp.exp(v2 - v1)
    gate1 = grp_prob / (1.0 + e21)
    gate2 = grp_prob * e21 / (1.0 + e21)

    hot1 = lane == i1
    hot2 = lane == i2
    onehot = jnp.where(hot1 | hot2, 1.0, 0.0)
    earlier = jnp.dot(ls_ref[...], onehot.astype(BF16), preferred_element_type=F32)
    chunks = jnp.ceil(jnp.sum(onehot, axis=0, keepdims=True) * (1.0 / MOE_CHUNK))
    chunks8 = jnp.broadcast_to(chunks, (SUBLANE, LANE))
    lstart = jnp.dot(chunks8.astype(BF16), us_ref[...], preferred_element_type=F32) * MOE_CHUNK
    pos = earlier + lstart[0:1]
    lpos1 = jnp.sum(jnp.where(hot1, pos, 0.0), axis=-1, keepdims=True)
    lpos2 = jnp.sum(jnp.where(hot2, pos, 0.0), axis=-1, keepdims=True)

    count = chunks8 * MOE_CHUNK
    row = lax.broadcasted_iota(jnp.int32, (SUBLANE, LANE), 0)
    tab_ref[...] = jnp.where(row == TAB_COUNT, count,
                             jnp.where(row == TAB_BEFORE, before_ref[...],
                                       jnp.where(row == TAB_LSTART, lstart, 0.0)))
    before_ref[...] = before_ref[...] + count

    info = jnp.zeros((tm, LANE), F32)
    for k, col in ((INFO_GATE, gate1), (INFO_GATE + 1, gate2), (INFO_LPOS, lpos1), (INFO_LPOS + 1, lpos2)):
        info = jnp.where(lane == k, col, info)
    info_ref[0] = info


def _outproj(mixp, mixh, x, w_out, gain, wr_split, rbias, tm, n_groups, epg):
    B, S, D = x.shape
    n_sl = D // LANE
    lstrict = jnp.asarray(np.tril(np.ones((tm, tm), np.float32), -1), BF16)
    ustrict = jnp.asarray(np.triu(np.ones((LANE, LANE), np.float32), 1), BF16)
    const = lambda b, i: (0, 0)
    tile = lambda b, i: (b * (S // tm) + i, 0)
    return pl.pallas_call(
        functools.partial(_outproj_body, n_groups=n_groups, epg=epg),
        grid=(B, S // tm),
        in_specs=[
            pl.BlockSpec((1, mixp.shape[1], tm, LANE), lambda b, i: (b, 0, i, 0)),
            pl.BlockSpec((1, mixh.shape[1], tm, LANE), lambda b, i: (b, 0, i, 0)),
            pl.BlockSpec((1, tm, D), lambda b, i: (b, i, 0)),
            pl.BlockSpec(w_out.shape, const),
            pl.BlockSpec((1, D), const),
            pl.BlockSpec((D, 2 * LANE), const),
            pl.BlockSpec((1, LANE), const),
            pl.BlockSpec((tm, tm), const),
            pl.BlockSpec((LANE, LANE), const),
        ],
        out_specs=[
            pl.BlockSpec((1, tm, D), lambda b, i: (b, i, 0)),
            pl.BlockSpec((tm * n_sl, LANE), tile),
            pl.BlockSpec((1, tm, LANE), lambda b, i: (b, i, 0)),
            pl.BlockSpec((SUBLANE, LANE), tile),
        ],
        out_shape=[
            jax.ShapeDtypeStruct((B, S, D), F32),
            jax.ShapeDtypeStruct((B * S * n_sl, LANE), F32),
            jax.ShapeDtypeStruct((B, S, LANE), F32),
            jax.ShapeDtypeStruct((B * (S // tm) * SUBLANE, LANE), F32),
        ],
        scratch_shapes=[pltpu.VMEM((SUBLANE, LANE), F32)],
        compiler_params=_params(2),
        name="outproj",
    )(mixp, mixh, x, w_out, gain, wr_split, rbias, lstrict, ustrict)


def _dispatch_body(nloc_ref, pad_ref, p1_ref, p2_ref, dst_ref, xn_ref, xs_ref, buf_a, buf_b, zbuf, sem,
                   *, tm, n_sl, n_ch, n_fix):
    i = pl.program_id(0)
    last = pl.num_programs(0) - 1
    piece = MOE_CHUNK * n_sl
    stride = _index_stride(n_ch)

    def chunk_copy(buf, j, chunk, s):
        d = pl.multiple_of(chunk * piece, piece)
        return pltpu.make_async_copy(buf.at[pl.ds(j * piece, piece)], xs_ref.at[pl.ds(d, piece)], sem.at[s])

    def start(buf, half, s):
        n_loc = nloc_ref[2 * i + half]
        for j in range(n_ch):
            copy = chunk_copy(buf, j, dst_ref[half * stride + j], s)
            if j < n_fix:
                copy.start()
            else:
                pl.when(j < n_loc)(copy.start)

    def wait_chunks(buf, n, s):
        pltpu.make_async_copy(buf.at[pl.ds(0, n * piece)], xs_ref.at[pl.ds(0, n * piece)], sem.at[s]).wait()

    def wait(buf, tile, s):
        n_loc = nloc_ref[tile]
        wait_chunks(buf, n_fix, s)
        for j in range(n_fix, n_ch):
            pl.when(j < n_loc)(functools.partial(wait_chunks, buf, 1, s))

    def sort(buf, half):
        buf[...] = jnp.zeros_like(buf)

        def move(t, carry):
            row = xn_ref[pl.ds(pl.multiple_of((half * tm + t) * n_sl, n_sl), n_sl), :]
            buf[pl.ds(pl.multiple_of(p1_ref[half * tm + t] * n_sl, n_sl), n_sl), :] = row
            buf[pl.ds(pl.multiple_of(p2_ref[half * tm + t] * n_sl, n_sl), n_sl), :] = row
            return carry

        lax.fori_loop(0, tm, move, 0, unroll=8)

    @pl.when(i == 0)
    def _():
        zbuf[...] = jnp.zeros_like(zbuf)
        n_listed = pad_ref.shape[0] - 1
        n_chunks = xs_ref.shape[0] // piece

        def fill(k, carry):
            pl.when(pad_ref[k] >= 0)(chunk_copy(zbuf, 0, jnp.maximum(pad_ref[k], 0), 2).start)
            return carry

        def filled(k, carry):
            pl.when(pad_ref[k] >= 0)(functools.partial(wait_chunks, zbuf, 1, 2))
            return carry

        def fill_tail(c, carry):
            chunk_copy(zbuf, 0, c, 2).start()
            return carry

        def filled_tail(c, carry):
            wait_chunks(zbuf, 1, 2)
            return carry

        lax.fori_loop(0, n_listed, fill, 0)
        lax.fori_loop(pad_ref[n_listed], n_chunks, fill_tail, 0)
        lax.fori_loop(0, n_listed, filled, 0)
        lax.fori_loop(pad_ref[n_listed], n_chunks, filled_tail, 0)

    @pl.when(i > 0)
    def _():
        wait(buf_a, 2 * i - 2, 0)

    sort(buf_a, 0)
    start(buf_a, 0, 0)

    @pl.when(i > 0)
    def _():
        wait(buf_b, 2 * i - 1, 1)

    sort(buf_b, 1)
    start(buf_b, 1, 1)

    @pl.when(i == last)
    def _():
        wait(buf_a, 2 * i, 0)
        wait(buf_b, 2 * i + 1, 1)


def _dispatch(n_local, pad_chunks, lpos1, lpos2, chunk_dst, xn_rows, n_slot_chunks, tm, cap, n_sl):
    n_tiles = lpos1.shape[0] // tm
    assert n_tiles % 2 == 0
    n_ch = cap // MOE_CHUNK
    stride = _index_stride(n_ch)
    piece = MOE_CHUNK * n_sl
    grid_spec = pltpu.PrefetchScalarGridSpec(
        num_scalar_prefetch=2,
        grid=(n_tiles // 2,),
        in_specs=[
            pl.BlockSpec((2 * tm,), lambda i, nl, pc: (i,), memory_space=pltpu.SMEM),
            pl.BlockSpec((2 * tm,), lambda i, nl, pc: (i,), memory_space=pltpu.SMEM),
            pl.BlockSpec((2 * stride,), lambda i, nl, pc: (i,), memory_space=pltpu.SMEM),
            pl.BlockSpec((2 * tm * n_sl, LANE), lambda i, nl, pc: (i, 0)),
        ],
        out_specs=pl.BlockSpec(memory_space=pl.ANY),
        scratch_shapes=[pltpu.VMEM((cap * n_sl, LANE), F32), pltpu.VMEM((cap * n_sl, LANE), F32),
                        pltpu.VMEM((piece, LANE), F32), pltpu.SemaphoreType.DMA((3,))],
    )
    return pl.pallas_call(
        functools.partial(_dispatch_body, tm=tm, n_sl=n_sl, n_ch=n_ch, n_fix=TOP_K * tm // MOE_CHUNK),
        grid_spec=grid_spec,
        out_shape=jax.ShapeDtypeStruct((n_slot_chunks * piece, LANE), F32),
        compiler_params=_params(1),
        name="dispatch",
    )(n_local, pad_chunks, lpos1, lpos2, _pad_lists(chunk_dst, n_ch), xn_rows)


def _start_rows(idx_ref, first, n, src_ref, dst_ref, sem, rows_per):
    for r in range(n):
        s = pl.multiple_of(idx_ref[first + r] * rows_per, rows_per)
        pltpu.make_async_copy(src_ref.at[pl.ds(s, rows_per)], dst_ref.at[pl.ds(r * rows_per, rows_per)], sem).start()


def _wait_rows(n, src_ref, dst_ref, sem, rows_per):
    total = n * rows_per
    pltpu.make_async_copy(src_ref.at[pl.ds(0, total)], dst_ref.at[pl.ds(0, total)], sem).wait()


def _index_stride(n):
    return max(LANE, pl.next_power_of_2(n))


def _pad_lists(idx, n):
    return jnp.pad(idx.reshape(-1, n), ((0, 0), (0, _index_stride(n) - n))).reshape(-1)


def _slab_rows(ref, n, n_sl, first=0):
    return jnp.concatenate([ref[pl.ds(first + j, n, stride=n_sl), :] for j in range(n_sl)], axis=-1)


def _experts_body(bexp_ref, nused_ref, xs_ref, w1a_ref, w3a_ref, w2a_ref, w1b_ref, w3b_ref, w2b_ref, y_ref, *, n_sl):
    i = pl.program_id(0)
    blk = EXPERT_ROWS

    def mlp(w1_ref, w3_ref, w2_ref, half):
        first = half * blk * n_sl
        xb = _slab_rows(xs_ref, blk, n_sl, first).astype(BF16)
        a = jnp.dot(xb, w1_ref[0], preferred_element_type=F32)
        h = (a * jax.nn.sigmoid(a)) * jnp.dot(xb, w3_ref[0], preferred_element_type=F32)
        y = jnp.dot(h.astype(BF16), w2_ref[0], preferred_element_type=F32)
        for j in range(n_sl):
            y_ref[pl.ds(first + j, blk, stride=n_sl), :] = y[:, j * LANE:(j + 1) * LANE]

    @pl.when(2 * i < nused_ref[0])
    def _():
        mlp(w1a_ref, w3a_ref, w2a_ref, 0)
        mlp(w1b_ref, w3b_ref, w2b_ref, 1)

    @pl.when(2 * i >= nused_ref[0])
    def _():
        y_ref[...] = jnp.zeros_like(y_ref)


def _experts(block_expert, n_used, xs_rows, w1, w3, w2, n_sl):
    n_blocks = block_expert.shape[0]
    assert n_blocks % 2 == 0
    blk = EXPERT_ROWS
    _, D, DE = w1.shape

    def wspec(shape, half):
        return pl.BlockSpec(shape, lambda i, be, nu, half=half: (be[2 * i + half], 0, 0))

    grid_spec = pltpu.PrefetchScalarGridSpec(
        num_scalar_prefetch=2,
        grid=(n_blocks // 2,),
        in_specs=[
            pl.BlockSpec((2 * blk * n_sl, LANE), lambda i, be, nu: (jnp.minimum(i, nu[0] // 2 - 1), 0)),
            wspec((1, D, DE), 0), wspec((1, D, DE), 0), wspec((1, DE, D), 0),
            wspec((1, D, DE), 1), wspec((1, D, DE), 1), wspec((1, DE, D), 1),
        ],
        out_specs=pl.BlockSpec((2 * blk * n_sl, LANE), lambda i, be, nu: (i, 0)),
    )
    return pl.pallas_call(
        functools.partial(_experts_body, n_sl=n_sl),
        grid_spec=grid_spec,
        out_shape=jax.ShapeDtypeStruct((n_blocks * blk * n_sl, LANE), F32),
        compiler_params=_params(1),
        name="experts",
    )(block_expert, n_used, xs_rows, w1, w3, w2, w1, w3, w2)


def _final_body(p1_ref, p2_ref, src_ref, nxt_ref, x2_ref, info_ref, gain_ref, y_ref, o_ref,
                ybuf_a, ybuf_b, tok1, tok2, sem, *, n_sl, tm, n_ch):
    i = pl.program_id(0)
    piece = MOE_CHUNK * n_sl

    def combine(ybuf, half):
        def unsort(t, carry):
            dst = pl.ds(pl.multiple_of(t * n_sl, n_sl), n_sl)
            tok1[dst, :] = ybuf[pl.ds(pl.multiple_of(p1_ref[half * tm + t] * n_sl, n_sl), n_sl), :]
            tok2[dst, :] = ybuf[pl.ds(pl.multiple_of(p2_ref[half * tm + t] * n_sl, n_sl), n_sl), :]
            return carry

        lax.fori_loop(0, tm, unsort, 0, unroll=8)
        rows = pl.ds(half * tm, tm)
        info = info_ref[rows, :]
        moe = (_slab_rows(tok1, tm, n_sl) * info[:, INFO_GATE:INFO_GATE + 1]
               + _slab_rows(tok2, tm, n_sl) * info[:, INFO_GATE + 1:INFO_GATE + 2])
        o_ref[rows, :] = _rms(x2_ref[rows, :] + moe, gain_ref[...])

    @pl.when(i == 0)
    def _():
        _start_rows(src_ref, 0, n_ch, y_ref, ybuf_a, sem.at[0], piece)

    _start_rows(src_ref, _index_stride(n_ch), n_ch, y_ref, ybuf_b, sem.at[1], piece)
    _wait_rows(n_ch, y_ref, ybuf_a, sem.at[0], piece)
    combine(ybuf_a, 0)
    _start_rows(nxt_ref, 0, n_ch, y_ref, ybuf_a, sem.at[0], piece)
    _wait_rows(n_ch, y_ref, ybuf_b, sem.at[1], piece)
    combine(ybuf_b, 1)

    @pl.when(i == pl.num_programs(0) - 1)
    def _():
        _wait_rows(n_ch, y_ref, ybuf_a, sem.at[0], piece)


def _final(lpos1, lpos2, y_src, x2, info, gain, y_rows, tm, cap, n_sl):
    T, D = x2.shape
    assert T % (2 * tm) == 0
    steps = T // (2 * tm)
    n_ch = cap // MOE_CHUNK
    stride = _index_stride(n_ch)
    y_src = _pad_lists(y_src, n_ch)
    cur = lambda i: (i,)
    nxt = lambda i: (jnp.minimum(2 * i + 2, 2 * steps - 1),)
    ybuf = pltpu.VMEM((cap * n_sl, LANE), F32)
    tbuf = pltpu.VMEM((tm * n_sl, LANE), F32)
    return pl.pallas_call(
        functools.partial(_final_body, n_sl=n_sl, tm=tm, n_ch=n_ch),
        grid=(steps,),
        in_specs=[
            pl.BlockSpec((2 * tm,), cur, memory_space=pltpu.SMEM),
            pl.BlockSpec((2 * tm,), cur, memory_space=pltpu.SMEM),
            pl.BlockSpec((2 * stride,), cur, memory_space=pltpu.SMEM),
            pl.BlockSpec((stride,), nxt, memory_space=pltpu.SMEM),
            pl.BlockSpec((2 * tm, D), lambda i: (i, 0)),
            pl.BlockSpec((2 * tm, LANE), lambda i: (i, 0)),
            pl.BlockSpec((1, D), lambda i: (0, 0)),
            pl.BlockSpec(memory_space=pl.ANY),
        ],
        out_specs=pl.BlockSpec((2 * tm, D), lambda i: (i, 0)),
        out_shape=jax.ShapeDtypeStruct((T, D), F32),
        scratch_shapes=[ybuf, ybuf, tbuf, tbuf, pltpu.SemaphoreType.DMA((2,))],
        compiler_params=_params(1),
        name="final",
    )(lpos1, lpos2, y_src, y_src, x2, info, gain, y_rows)


def _tile(n, pref):
    return pref if n % pref == 0 else n


def _layer(x, p):
    B, S, D = x.shape
    T = B * S
    n_sl = D // LANE
    G = p["pool_w"].shape[0]
    E = p["w1"].shape[0]
    n_groups = p["n_groups"]
    tm = _tile(S, 512)
    n_tiles = T // tm

    u = _inproj(x, p["norm_mix"], p["w_in"], tm)
    mixp = _pool(u, p["pool_w"], p["pool_scale"])
    mixh = _hgrn(u, p["lb_f"], p["lb_b"], p["hg_gain"], G)
    x2, xn_rows, info, tab = _outproj(mixp, mixh, x, p["w_out"], p["norm_ffn"], p["wr_split"],
                                      p["rbias"], tm, n_groups, E // n_groups)

    blk = EXPERT_ROWS
    ch = MOE_CHUNK
    cap = TOP_K * tm + E * ch
    cap_ch = cap // ch
    n_blocks = -(-(T * TOP_K + n_tiles * E * (ch - 1)) // blk) + E + 1
    n_blocks += n_blocks % 2
    n_slot_chunks = n_blocks * blk // ch
    tab = tab.reshape(n_tiles, SUBLANE, LANE)[:, :, n_groups:n_groups + E].astype(jnp.int32)
    count, before, lstart = tab[:, TAB_COUNT], tab[:, TAB_BEFORE], tab[:, TAB_LSTART]
    total = before[-1] + count[-1]
    padded = (total + blk - 1) // blk * blk
    padded = padded.at[E - 1].add(blk * ((jnp.sum(padded) // blk) % 2))
    pend = jnp.cumsum(padded)
    pstart = pend - padded
    first_slot = jnp.arange(n_blocks, dtype=jnp.int32) * blk
    block_expert = jnp.minimum(jnp.sum(pend[None, :] <= first_slot[:, None], axis=1), E - 1).astype(jnp.int32)
    n_used = (pend[-1:] // blk).astype(jnp.int32)

    slot_chunk = (pstart[None, :] + before) // ch
    local_chunk = lstart // ch
    n_local = (lstart[:, -1] + count[:, -1]) // ch
    off = slot_chunk - local_chunk
    step = off - jnp.concatenate([jnp.zeros((n_tiles, 1), jnp.int32), off[:, :-1]], axis=1)
    steps = jnp.zeros((n_tiles, cap_ch + 1), jnp.int32).at[jnp.arange(n_tiles)[:, None], local_chunk].add(step)
    chunk_slot = jnp.arange(cap_ch, dtype=jnp.int32)[None, :] + jnp.cumsum(steps, axis=1)[:, :cap_ch]
    chunk_slot = jnp.clip(chunk_slot, 0, n_slot_chunks - 1).reshape(-1)
    k = jnp.arange(2 * blk // ch, dtype=jnp.int32)[None, :]
    pad_chunks = ((pstart + total) // ch)[:, None] + k
    pad_chunks = jnp.where(pad_chunks < (pend // ch)[:, None], pad_chunks, -1).reshape(-1)
    pad_chunks = jnp.concatenate([pad_chunks, pend[-1:] // ch])

    info2 = info.reshape(T, LANE)
    lpos = info2[:, INFO_LPOS:INFO_LPOS + TOP_K].astype(jnp.int32)
    xs_rows = _dispatch(n_local, pad_chunks, lpos[:, 0], lpos[:, 1], chunk_slot, xn_rows, n_slot_chunks, tm, cap, n_sl)
    y_rows = _experts(block_expert, n_used, xs_rows, p["w1"], p["w3"], p["w2"], n_sl)
    out = _final(lpos[:, 0], lpos[:, 1], chunk_slot, x2.reshape(T, D), info2, p["norm_final"], y_rows, tm, cap, n_sl)
    return out.reshape(B, S, D)


def kernel(x_prompt, x_sample, w_in, w_out, pool_w, pool_scale, hg_lb_fwd, hg_lb_bwd, hg_norm_gain, norm_mix, norm_ffn, router_group_w, router_group_b, router_expert_w, router_expert_b, expert_w1, expert_w3, expert_w2, norm_final):
    depth = w_in.shape[0]
    assert depth == 1, "the final norm is fused into the last layer's combine kernel; one layer supported"
    D = w_in.shape[1]
    hg_width = hg_lb_fwd.shape[1]
    dv = hg_norm_gain.shape[1]
    pg = pool_w.shape[2]
    assert dv == LANE and pg == LANE and D % LANE == 0
    H = hg_width // dv
    n_groups = router_group_w.shape[-1]
    E = router_expert_w.shape[-1]
    assert n_groups + E <= LANE

    lb_f = jnp.cumsum(jax.nn.softmax(hg_lb_fwd.astype(F32), axis=0), axis=0)
    lb_b = jnp.cumsum(jax.nn.softmax(hg_lb_bwd.astype(F32), axis=0), axis=0)
    l = 0
    wr = jnp.concatenate([router_group_w[l], router_expert_w[l]], axis=1).astype(F32)
    wr = jnp.pad(wr, ((0, 0), (0, LANE - wr.shape[1])))
    wr_hi = wr.astype(BF16)
    rbias = jnp.concatenate([router_group_b[l], router_expert_b[l]]).astype(F32)
    p = dict(
        n_groups=n_groups,
        w_in=w_in[l].astype(BF16), w_out=w_out[l].astype(BF16),
        pool_w=pool_w[l], pool_scale=pool_scale[l],
        lb_f=lb_f[l].reshape(H, 1, dv), lb_b=lb_b[l].reshape(H, 1, dv), hg_gain=hg_norm_gain[l].reshape(1, dv).astype(F32),
        norm_mix=norm_mix[l].reshape(1, D).astype(F32), norm_ffn=norm_ffn[l].reshape(1, D).astype(F32),
        norm_final=norm_final.reshape(1, D).astype(F32),
        wr_split=jnp.concatenate([wr_hi, (wr - wr_hi.astype(F32)).astype(BF16)], axis=1),
        rbias=jnp.pad(rbias, (0, LANE - rbias.shape[0])).reshape(1, LANE),
        w1=expert_w1[l].astype(BF16), w3=expert_w3[l].astype(BF16), w2=expert_w2[l].astype(BF16),
    )
    return (_layer(x_prompt, p), _layer(x_sample, p))
```

```python
import functools

import jax
import jax.numpy as jnp
import numpy as np
from jax import lax
from jax.experimental import pallas as pl
from jax.experimental.pallas import tpu as pltpu

F32 = jnp.float32
BF16 = jnp.bfloat16

EPS = 1e-6
POOL_WINDOWS = (2, 4, 8, 16)
TOP_K = 2

LANE = 128
SUBLANE = 8
VMEM_LIMIT = 56 * 1024 * 1024

HG_CHUNK = 32
HG_GROUP = 128
HG_UNROLL = 4
POOL_ROWS = 128
POOL_UNROLL = 4
EXPERT_ROWS = 256


def _params(n_axes):
    return pltpu.CompilerParams(dimension_semantics=("arbitrary",) * n_axes, vmem_limit_bytes=VMEM_LIMIT)


def _rms(x, gain):
    return x * lax.rsqrt(jnp.mean(x * x, axis=-1, keepdims=True) + EPS) * gain


def _inproj_body(x_ref, gain_ref, w_ref, u_ref, *, nc):
    n = _rms(x_ref[0], gain_ref[...]).astype(BF16)
    per = nc // LANE
    for c in range(w_ref.shape[1] // nc):
        r = jnp.dot(n, w_ref[:, c * nc:(c + 1) * nc], preferred_element_type=F32)
        for j in range(per):
            u_ref[0, c * per + j] = r[:, j * LANE:(j + 1) * LANE].astype(BF16)


def _inproj(x, gain, w, tm):
    B, S, D = x.shape
    cols = w.shape[1]
    return pl.pallas_call(
        functools.partial(_inproj_body, nc=4 * LANE),
        grid=(B, S // tm),
        in_specs=[
            pl.BlockSpec((1, tm, D), lambda b, i: (b, i, 0)),
            pl.BlockSpec((1, D), lambda b, i: (0, 0)),
            pl.BlockSpec((D, cols), lambda b, i: (0, 0)),
        ],
        out_specs=pl.BlockSpec((1, cols // LANE, tm, LANE), lambda b, i: (b, 0, i, 0)),
        out_shape=jax.ShapeDtypeStruct((B, cols // LANE, S, LANE), BF16),
        compiler_params=_params(2),
        name="inproj",
    )(x, gain, w)


def _band_matrices(rows):
    t = np.arange(rows)[:, None]
    s = np.arange(rows)[None, :]
    out = np.zeros((len(POOL_WINDOWS), 3, rows, rows), np.float32)
    for gi, w in enumerate(POOL_WINDOWS):
        for k, shift in enumerate((-rows, 0, rows)):
            pos = s + shift
            out[gi, k] = (pos >= t - w // 2) & (pos < t + w // 2)
    return out


def _pool_body(half_ref, u_ref, band_ref, pw_ref, sc_ref, o_ref, *, seq, rows, unroll):
    nt = seq // rows
    h = half_ref[pl.program_id(1)]

    def window_sum(i):
        r0 = pl.multiple_of(i * rows, rows)
        rp = pl.multiple_of(jnp.maximum(i - 1, 0) * rows, rows)
        rn = pl.multiple_of(jnp.minimum(i + 1, nt - 1) * rows, rows)
        xc = u_ref[0, 0, pl.ds(r0, rows), :]
        s = jnp.dot(band_ref[0, 1], xc, preferred_element_type=F32)
        sp = jnp.dot(band_ref[0, 0], u_ref[0, 0, pl.ds(rp, rows), :], preferred_element_type=F32)
        sn = jnp.dot(band_ref[0, 2], u_ref[0, 0, pl.ds(rn, rows), :], preferred_element_type=F32)
        return r0, xc, s + jnp.where(i > 0, sp, 0.0) + jnp.where(i < nt - 1, sn, 0.0)

    def pooled(r0, xc, s):
        t = r0 + lax.broadcasted_iota(jnp.int32, (rows, LANE), 0)
        cnt = (jnp.minimum(t + h, seq) - jnp.maximum(t - h, 0)).astype(F32)
        return (s / cnt - xc.astype(F32)).astype(BF16)

    def tiles(j, carry):
        sums = [window_sum(j * unroll + k) for k in range(unroll)]
        pools = [pooled(*a) for a in sums]
        ys = [jnp.dot(pv, pw_ref[0], preferred_element_type=F32) * sc_ref[0] for pv in pools]
        for (r0, _, _), y in zip(sums, ys):
            o_ref[0, 0, pl.ds(r0, rows), :] = y.astype(BF16)
        return carry

    lax.fori_loop(0, nt // unroll, tiles, 0)


def _pool(u, pool_w, pool_scale):
    B, _, S, _ = u.shape
    G = pool_w.shape[0]
    rows = min(POOL_ROWS, S)
    band = jnp.asarray(_band_matrices(rows), BF16)
    halves = jnp.asarray([w // 2 for w in POOL_WINDOWS], jnp.int32)
    grid_spec = pltpu.PrefetchScalarGridSpec(
        num_scalar_prefetch=1,
        grid=(B, G),
        in_specs=[
            pl.BlockSpec((1, 1, S, LANE), lambda b, g, h: (b, g, 0, 0)),
            pl.BlockSpec((1, 3, rows, rows), lambda b, g, h: (g, 0, 0, 0)),
            pl.BlockSpec((1, LANE, LANE), lambda b, g, h: (g, 0, 0)),
            pl.BlockSpec((1, 1, LANE), lambda b, g, h: (g, 0, 0)),
        ],
        out_specs=pl.BlockSpec((1, 1, S, LANE), lambda b, g, h: (b, g, 0, 0)),
    )
    return pl.pallas_call(
        functools.partial(_pool_body, seq=S, rows=rows, unroll=POOL_UNROLL if (S // rows) % POOL_UNROLL == 0 else 1),
        grid_spec=grid_spec,
        out_shape=jax.ShapeDtypeStruct((B, G, S, LANE), BF16),
        compiler_params=_params(2),
        name="pool",
    )(halves, u, band, pool_w.astype(BF16), pool_scale.reshape(G, 1, LANE).astype(F32))


def _hg_gates(q_ref, f_ref, v_ref, r0, lb):
    rows = pl.ds(r0, HG_GROUP)
    q = q_ref[0, 0, rows, :].astype(F32)
    q = q * jax.nn.sigmoid(q)
    f = lb + (1.0 - lb) * jax.nn.sigmoid(f_ref[0, 0, rows, :].astype(F32))
    g = jnp.log(f)
    g_hi = g.astype(BF16)
    g_lo = (g - g_hi.astype(F32)).astype(BF16)
    return q, 1.0 - f, v_ref[0, 0, rows, :], g_hi, g_lo


def _block_rows(rows):
    return jnp.concatenate([jnp.broadcast_to(r, (HG_CHUNK, LANE)) for r in rows], axis=0)


def _hg_levels(reverse):
    assert HG_GROUP == 4 * HG_CHUNK
    C = HG_CHUNK
    if reverse:
        mid = [c * C + C // 2 for c in range(4)]
        level1 = [(0, 1, C), (2, 3, 3 * C)]
        level2 = [((0, 1), (2, 3), 2 * C)]
        end = 0
    else:
        mid = [c * C + C // 2 - 1 for c in range(4)]
        level1 = [(1, 0, C - 1), (3, 2, 3 * C - 1)]
        level2 = [((2, 3), (0, 1), 2 * C - 1)]
        end = HG_GROUP - 1
    return mid, level1, level2, end


def _hg_decays(q, k, b, *, reverse):
    mid_rows, level1, level2, end_row = _hg_levels(reverse)
    at = lambda r: b[r][None]
    mid = [at(r) for r in mid_rows]
    end = at(end_row)
    b_mid = _block_rows(mid)
    q_mid = q * jnp.exp(b - b_mid)
    k_mid = k * jnp.exp(b_mid - b)
    zero = jnp.zeros((1, LANE), F32)
    q1, k1, q2, k2 = ([zero] * 4 for _ in range(4))
    for rb, cb, ref in level1:
        q1[rb] = jnp.exp(mid[rb] - at(ref))
        k1[cb] = jnp.exp(at(ref) - mid[cb])
    for rbs, cbs, ref in level2:
        for rb in rbs:
            q2[rb] = jnp.exp(mid[rb] - at(ref))
        for cb in cbs:
            k2[cb] = jnp.exp(at(ref) - mid[cb])
    scale = lambda x, rows: (x * _block_rows(rows)).astype(BF16)
    return dict(
        q_dec=scale(q_mid, [jnp.exp(m) for m in mid]), k_st=scale(k_mid, [jnp.exp(end - m) for m in mid]),
        q0=q_mid.astype(BF16), k0=k_mid.astype(BF16), q1=scale(q_mid, q1), k1=scale(k_mid, k1),
        q2=scale(q_mid, q2), k2=scale(k_mid, k2), dec=jnp.exp(end))


def _hg_masks(reverse):
    C = HG_CHUNK
    R = HG_GROUP
    ri = lax.broadcasted_iota(jnp.int32, (R, R), 0)
    ci = lax.broadcasted_iota(jnp.int32, (R, R), 1)
    rb, cb = ri // C, ci // C
    _, level1, level2, _ = _hg_levels(reverse)
    m0 = (rb == cb) & ((ci >= ri) if reverse else (ci <= ri))
    m1 = functools.reduce(lambda a, e: a | ((rb == e[0]) & (cb == e[1])), level1, jnp.zeros((R, R), jnp.bool_))
    (rbs, cbs, _), = level2
    m2 = ((rb == rbs[0]) | (rb == rbs[1])) & ((cb == cbs[0]) | (cb == cbs[1]))
    return m0, m1, m2


def _hgrn_body(q_ref, ff_ref, fb_ref, v_ref, g_ref, lbf_ref, lbb_ref, gain_ref, tri_ref,
               o_ref, of_scr, ob_scr, *, seq, rows, unroll):
    R = HG_GROUP
    n_groups = seq // R
    lb = (lbf_ref[0], lbb_ref[0])
    f_refs = (ff_ref, fb_ref)
    scr = (of_scr, ob_scr)
    masks = (_hg_masks(False), _hg_masks(True))
    contract_last = (((1,), (1,)), ((), ()))
    contract_first = (((0,), (0,)), ((), ()))

    def step(j, carry):
        streams = []
        for d in range(2):
            for i in range(unroll):
                gi = j * unroll + i
                streams.append((d, pl.multiple_of((gi if d == 0 else n_groups - 1 - gi) * R, R)))
        gates = [_hg_gates(q_ref, f_refs[d], v_ref, r0, lb[d]) for d, r0 in streams]
        cums = [jnp.dot(tri_ref[d], g_hi, preferred_element_type=F32)
                + jnp.dot(tri_ref[d], g_lo, preferred_element_type=F32)
                for (d, _), (_, _, _, g_hi, g_lo) in zip(streams, gates)]
        ops = [_hg_decays(q, k, b, reverse=(d == 1)) for (d, _), (q, k, _, _, _), b in zip(streams, gates, cums)]
        score = lambda a, b: lax.dot_general(a, b, contract_last, preferred_element_type=F32)
        levels = [(score(o["q0"], o["k0"]), score(o["q1"], o["k1"]), score(o["q2"], o["k2"])) for o in ops]
        kvs = [lax.dot_general(v, o["k_st"], contract_first, preferred_element_type=F32)
               for (_, _, v, _, _), o in zip(gates, ops)]
        intra = []
        for (d, _), (s0, s1, s2), (_, _, v, _, _) in zip(streams, levels, gates):
            m0, m1, m2 = masks[d]
            s = jnp.where(m0, s0, jnp.where(m1, s1, jnp.where(m2, s2, 0.0)))
            intra.append(jnp.dot(s.astype(BF16), v, preferred_element_type=F32))
        states = list(carry)
        for n, (d, r0) in enumerate(streams):
            inter = lax.dot_general(ops[n]["q_dec"], states[d].astype(BF16), contract_last,
                                    preferred_element_type=F32)
            states[d] = states[d] * ops[n]["dec"] + kvs[n]
            scr[d][pl.ds(r0, R), :] = intra[n] + inter
        return tuple(states)

    zero = jnp.zeros((LANE, LANE), F32)
    lax.fori_loop(0, n_groups // unroll, step, (zero, zero))

    def finish(i, carry):
        r = pl.ds(pl.multiple_of(i * rows, rows), rows)
        o = of_scr[r, :] + ob_scr[r, :]
        o = _rms(o, gain_ref[...])
        gate = g_ref[0, 0, r, :].astype(F32)
        o_ref[0, 0, r, :] = (o * (gate * jax.nn.sigmoid(gate))).astype(BF16)
        return carry

    lax.fori_loop(0, seq // rows, finish, 0)


def _hgrn(u, lb_f, lb_b, gain, n_pool):
    B, _, S, _ = u.shape
    H = lb_f.shape[0]
    C = HG_CHUNK
    R = HG_GROUP
    assert S % R == 0
    low = np.tril(np.ones((R, R), np.float32))
    tri = jnp.asarray(np.stack([low, low.T]), BF16)
    rows = min(256, S)
    unroll = HG_UNROLL if (S // R) % HG_UNROLL == 0 else 1

    def slab(k):
        return pl.BlockSpec((1, 1, S, LANE), lambda b, h, k=k: (b, n_pool + k * H + h, 0, 0))

    head_vec = pl.BlockSpec((1, 1, LANE), lambda b, h: (h, 0, 0))
    return pl.pallas_call(
        functools.partial(_hgrn_body, seq=S, rows=rows, unroll=unroll),
        grid=(B, H),
        in_specs=[slab(0), slab(1), slab(2), slab(3), slab(4), head_vec, head_vec,
                  pl.BlockSpec((1, LANE), lambda b, h: (0, 0)),
                  pl.BlockSpec((2, R, R), lambda b, h: (0, 0, 0))],
        out_specs=pl.BlockSpec((1, 1, S, LANE), lambda b, h: (b, h, 0, 0)),
        out_shape=jax.ShapeDtypeStruct((B, H, S, LANE), BF16),
        scratch_shapes=[pltpu.VMEM((S, LANE), F32), pltpu.VMEM((S, LANE), F32)],
        compiler_params=_params(2),
        name="hgrn",
    )(u, u, u, u, u, lb_f, lb_b, gain, tri)


INFO_GATE = 0
INFO_LPOS = 2

TAB_COUNT = 0
TAB_BEFORE = 1
TAB_LSTART = 2

MOE_CHUNK = 8


def _outproj_body(mp_ref, mh_ref, x_ref, wo_ref, gain_ref, wr_ref, rb_ref, ls_ref, us_ref,
                  x2_ref, xn_ref, info_ref, tab_ref, before_ref, *, n_groups, epg):
    tm = x_ref.shape[1]

    @pl.when((pl.program_id(0) == 0) & (pl.program_id(1) == 0))
    def _():
        before_ref[...] = jnp.zeros_like(before_ref)

    mix = jnp.concatenate([mp_ref[0, j] for j in range(mp_ref.shape[1])]
                          + [mh_ref[0, j] for j in range(mh_ref.shape[1])], axis=-1)
    x2 = x_ref[0] + jnp.dot(mix, wo_ref[...], preferred_element_type=F32)
    x2_ref[0] = x2
    xn = _rms(x2, gain_ref[...])
    for j in range(xn.shape[1] // LANE):
        xn_ref[pl.ds(j, tm, stride=xn.shape[1] // LANE), :] = xn[:, j * LANE:(j + 1) * LANE]

    xh = xn.astype(BF16)
    xl = (xn - xh.astype(F32)).astype(BF16)
    hh_hl = jnp.dot(xh, wr_ref[...], preferred_element_type=F32)
    logits = (hh_hl[:, :LANE] + hh_hl[:, LANE:]
              + jnp.dot(xl, wr_ref[:, :LANE], preferred_element_type=F32)) + rb_ref[...]

    lane = lax.broadcasted_iota(jnp.int32, (tm, LANE), 1)
    neg = jnp.float32(-jnp.inf)
    big = jnp.int32(LANE)

    def top(vals):
        m = jnp.max(vals, axis=-1, keepdims=True)
        return m, jnp.min(jnp.where(vals == m, lane, big), axis=-1, keepdims=True)

    glog = jnp.where(lane < n_groups, logits, neg)
    gmax, grp = top(glog)
    grp_prob = 1.0 / jnp.sum(jnp.exp(glog - gmax), axis=-1, keepdims=True)
    e_lo = n_groups + grp * epg
    elog = jnp.where((lane >= e_lo) & (lane < e_lo + epg), logits, neg)
    v1, i1 = top(elog)
    v2, i2 = top(jnp.where(lane == i1, neg, elog))
    e21 = jnp.exp(v2 - v1)
    gate1 = grp_prob / (1.0 + e21)
    gate2 = grp_prob * e21 / (1.0 + e21)

    hot1 = lane == i1
    hot2 = lane == i2
    onehot = jnp.where(hot1 | hot2, 1.0, 0.0)
    earlier = jnp.dot(ls_ref[...], onehot.astype(BF16), preferred_element_type=F32)
    chunks = jnp.ceil(jnp.sum(onehot, axis=0, keepdims=True) * (1.0 / MOE_CHUNK))
    chunks8 = jnp.broadcast_to(chunks, (SUBLANE, LANE))
    lstart = jnp.dot(chunks8.astype(BF16), us_ref[...], preferred_element_type=F32) * MOE_CHUNK
    pos = earlier + lstart[0:1]
    lpos1 = jnp.sum(jnp.where(hot1, pos, 0.0), axis=-1, keepdims=True)
    lpos2 = jnp.sum(jnp.where(hot2, pos, 0.0), axis=-1, keepdims=True)

    count = chunks8 * MOE_CHUNK
    row = lax.broadcasted_iota(jnp.int32, (SUBLANE, LANE), 0)
    tab_ref[...] = jnp.where(row == TAB_COUNT, count,
                             jnp.where(row == TAB_BEFORE, before_ref[...],
                                       jnp.where(row == TAB_LSTART, lstart, 0.0)))
    before_ref[...] = before_ref[...] + count

    info = jnp.zeros((tm, LANE), F32)
    for k, col in ((INFO_GATE, gate1), (INFO_GATE + 1, gate2), (INFO_LPOS, lpos1), (INFO_LPOS + 1, lpos2)):
        info = jnp.where(lane == k, col, info)
    info_ref[0] = info


def _outproj(mixp, mixh, x, w_out, gain, wr_split, rbias, tm, n_groups, epg):
    B, S, D = x.shape
    n_sl = D // LANE
    lstrict = jnp.asarray(np.tril(np.ones((tm, tm), np.float32), -1), BF16)
    ustrict = jnp.asarray(np.triu(np.ones((LANE, LANE), np.float32), 1), BF16)
    const = lambda b, i: (0, 0)
    tile = lambda b, i: (b * (S // tm) + i, 0)
    return pl.pallas_call(
        functools.partial(_outproj_body, n_groups=n_groups, epg=epg),
        grid=(B, S // tm),
        in_specs=[
            pl.BlockSpec((1, mixp.shape[1], tm, LANE), lambda b, i: (b, 0, i, 0)),
            pl.BlockSpec((1, mixh.shape[1], tm, LANE), lambda b, i: (b, 0, i, 0)),
            pl.BlockSpec((1, tm, D), lambda b, i: (b, i, 0)),
            pl.BlockSpec(w_out.shape, const),
            pl.BlockSpec((1, D), const),
            pl.BlockSpec((D, 2 * LANE), const),
            pl.BlockSpec((1, LANE), const),
            pl.BlockSpec((tm, tm), const),
            pl.BlockSpec((LANE, LANE), const),
        ],
        out_specs=[
            pl.BlockSpec((1, tm, D), lambda b, i: (b, i, 0)),
            pl.BlockSpec((tm * n_sl, LANE), tile),
            pl.BlockSpec((1, tm, LANE), lambda b, i: (b, i, 0)),
            pl.BlockSpec((SUBLANE, LANE), tile),
        ],
        out_shape=[
            jax.ShapeDtypeStruct((B, S, D), F32),
            jax.ShapeDtypeStruct((B * S * n_sl, LANE), F32),
            jax.ShapeDtypeStruct((B, S, LANE), F32),
            jax.ShapeDtypeStruct((B * (S // tm) * SUBLANE, LANE), F32),
        ],
        scratch_shapes=[pltpu.VMEM((SUBLANE, LANE), F32)],
        compiler_params=_params(2),
        name="outproj",
    )(mixp, mixh, x, w_out, gain, wr_split, rbias, lstrict, ustrict)


def _dispatch_body(nloc_ref, pad_ref, p1_ref, p2_ref, dst_ref, xn_ref, xs_ref, buf_a, buf_b, zbuf, sem,
                   *, tm, n_sl, n_ch, n_fix):
    i = pl.program_id(0)
    last = pl.num_programs(0) - 1
    piece = MOE_CHUNK * n_sl
    stride = _index_stride(n_ch)

    def chunk_copy(buf, j, chunk, s):
        d = pl.multiple_of(chunk * piece, piece)
        return pltpu.make_async_copy(buf.at[pl.ds(j * piece, piece)], xs_ref.at[pl.ds(d, piece)], sem.at[s])

    def start(buf, half, s):
        n_loc = nloc_ref[2 * i + half]
        for j in range(n_ch):
            copy = chunk_copy(buf, j, dst_ref[half * stride + j], s)
            if j < n_fix:
                copy.start()
            else:
                pl.when(j < n_loc)(copy.start)

    def wait_chunks(buf, n, s):
        pltpu.make_async_copy(buf.at[pl.ds(0, n * piece)], xs_ref.at[pl.ds(0, n * piece)], sem.at[s]).wait()

    def wait(buf, tile, s):
        n_loc = nloc_ref[tile]
        wait_chunks(buf, n_fix, s)
        for j in range(n_fix, n_ch):
            pl.when(j < n_loc)(functools.partial(wait_chunks, buf, 1, s))

    def sort(buf, half):
        buf[...] = jnp.zeros_like(buf)

        def move(t, carry):
            row = xn_ref[pl.ds(pl.multiple_of((half * tm + t) * n_sl, n_sl), n_sl), :]
            buf[pl.ds(pl.multiple_of(p1_ref[half * tm + t] * n_sl, n_sl), n_sl), :] = row
            buf[pl.ds(pl.multiple_of(p2_ref[half * tm + t] * n_sl, n_sl), n_sl), :] = row
            return carry

        lax.fori_loop(0, tm, move, 0, unroll=8)

    @pl.when(i == 0)
    def _():
        zbuf[...] = jnp.zeros_like(zbuf)
        n_listed = pad_ref.shape[0] - 1
        n_chunks = xs_ref.shape[0] // piece

        def fill(k, carry):
            pl.when(pad_ref[k] >= 0)(chunk_copy(zbuf, 0, jnp.maximum(pad_ref[k], 0), 2).start)
            return carry

        def filled(k, carry):
            pl.when(pad_ref[k] >= 0)(functools.partial(wait_chunks, zbuf, 1, 2))
            return carry

        def fill_tail(c, carry):
            chunk_copy(zbuf, 0, c, 2).start()
            return carry

        def filled_tail(c, carry):
            wait_chunks(zbuf, 1, 2)
            return carry

        lax.fori_loop(0, n_listed, fill, 0)
        lax.fori_loop(pad_ref[n_listed], n_chunks, fill_tail, 0)
        lax.fori_loop(0, n_listed, filled, 0)
        lax.fori_loop(pad_ref[n_listed], n_chunks, filled_tail, 0)

    @pl.when(i > 0)
    def _():
        wait(buf_a, 2 * i - 2, 0)

    sort(buf_a, 0)
    start(buf_a, 0, 0)

    @pl.when(i > 0)
    def _():
        wait(buf_b, 2 * i - 1, 1)

    sort(buf_b, 1)
    start(buf_b, 1, 1)

    @pl.when(i == last)
    def _():
        wait(buf_a, 2 * i, 0)
        wait(buf_b, 2 * i + 1, 1)


def _dispatch(n_local, pad_chunks, lpos1, lpos2, chunk_dst, xn_rows, n_slot_chunks, tm, cap, n_sl):
    n_tiles = lpos1.shape[0] // tm
    assert n_tiles % 2 == 0
    n_ch = cap // MOE_CHUNK
    stride = _index_stride(n_ch)
    piece = MOE_CHUNK * n_sl
    grid_spec = pltpu.PrefetchScalarGridSpec(
        num_scalar_prefetch=2,
        grid=(n_tiles // 2,),
        in_specs=[
            pl.BlockSpec((2 * tm,), lambda i, nl, pc: (i,), memory_space=pltpu.SMEM),
            pl.BlockSpec((2 * tm,), lambda i, nl, pc: (i,), memory_space=pltpu.SMEM),
            pl.BlockSpec((2 * stride,), lambda i, nl, pc: (i,), memory_space=pltpu.SMEM),
            pl.BlockSpec((2 * tm * n_sl, LANE), lambda i, nl, pc: (i, 0)),
        ],
        out_specs=pl.BlockSpec(memory_space=pl.ANY),
        scratch_shapes=[pltpu.VMEM((cap * n_sl, LANE), F32), pltpu.VMEM((cap * n_sl, LANE), F32),
                        pltpu.VMEM((piece, LANE), F32), pltpu.SemaphoreType.DMA((3,))],
    )
    return pl.pallas_call(
        functools.partial(_dispatch_body, tm=tm, n_sl=n_sl, n_ch=n_ch, n_fix=TOP_K * tm // MOE_CHUNK),
        grid_spec=grid_spec,
        out_shape=jax.ShapeDtypeStruct((n_slot_chunks * piece, LANE), F32),
        compiler_params=_params(1),
        name="dispatch",
    )(n_local, pad_chunks, lpos1, lpos2, _pad_lists(chunk_dst, n_ch), xn_rows)


def _start_rows(idx_ref, first, n, src_ref, dst_ref, sem, rows_per):
    for r in range(n):
        s = pl.multiple_of(idx_ref[first + r] * rows_per, rows_per)
        pltpu.make_async_copy(src_ref.at[pl.ds(s, rows_per)], dst_ref.at[pl.ds(r * rows_per, rows_per)], sem).start()


def _wait_rows(n, src_ref, dst_ref, sem, rows_per):
    total = n * rows_per
    pltpu.make_async_copy(src_ref.at[pl.ds(0, total)], dst_ref.at[pl.ds(0, total)], sem).wait()


def _index_stride(n):
    return max(LANE, pl.next_power_of_2(n))


def _pad_lists(idx, n):
    return jnp.pad(idx.reshape(-1, n), ((0, 0), (0, _index_stride(n) - n))).reshape(-1)


def _slab_rows(ref, n, n_sl, first=0):
    return jnp.concatenate([ref[pl.ds(first + j, n, stride=n_sl), :] for j in range(n_sl)], axis=-1)


def _experts_body(bexp_ref, nused_ref, xs_ref, w1a_ref, w3a_ref, w2a_ref, w1b_ref, w3b_ref, w2b_ref, y_ref, *, n_sl):
    i = pl.program_id(0)
    blk = EXPERT_ROWS

    def mlp(w1_ref, w3_ref, w2_ref, half):
        first = half * blk * n_sl
        xb = _slab_rows(xs_ref, blk, n_sl, first).astype(BF16)
        a = jnp.dot(xb, w1_ref[0], preferred_element_type=F32)
        h = (a * jax.nn.sigmoid(a)) * jnp.dot(xb, w3_ref[0], preferred_element_type=F32)
        y = jnp.dot(h.astype(BF16), w2_ref[0], preferred_element_type=F32)
        for j in range(n_sl):
            y_ref[pl.ds(first + j, blk, stride=n_sl), :] = y[:, j * LANE:(j + 1) * LANE]

    @pl.when(2 * i < nused_ref[0])
    def _():
        mlp(w1a_ref, w3a_ref, w2a_ref, 0)
        mlp(w1b_ref, w3b_ref, w2b_ref, 1)

    @pl.when(2 * i >= nused_ref[0])
    def _():
        y_ref[...] = jnp.zeros_like(y_ref)


def _experts(block_expert, n_used, xs_rows, w1, w3, w2, n_sl):
    n_blocks = block_expert.shape[0]
    assert n_blocks % 2 == 0
    blk = EXPERT_ROWS
    _, D, DE = w1.shape

    def wspec(shape, half):
        return pl.BlockSpec(shape, lambda i, be, nu, half=half: (be[2 * i + half], 0, 0))

    grid_spec = pltpu.PrefetchScalarGridSpec(
        num_scalar_prefetch=2,
        grid=(n_blocks // 2,),
        in_specs=[
            pl.BlockSpec((2 * blk * n_sl, LANE), lambda i, be, nu: (jnp.minimum(i, nu[0] // 2 - 1), 0)),
            wspec((1, D, DE), 0), wspec((1, D, DE), 0), wspec((1, DE, D), 0),
            wspec((1, D, DE), 1), wspec((1, D, DE), 1), wspec((1, DE, D), 1),
        ],
        out_specs=pl.BlockSpec((2 * blk * n_sl, LANE), lambda i, be, nu: (i, 0)),
    )
    return pl.pallas_call(
        functools.partial(_experts_body, n_sl=n_sl),
        grid_spec=grid_spec,
        out_shape=jax.ShapeDtypeStruct((n_blocks * blk * n_sl, LANE), F32),
        compiler_params=_params(1),
        name="experts",
    )(block_expert, n_used, xs_rows, w1, w3, w2, w1, w3, w2)


def _final_body(p1_ref, p2_ref, src_ref, nxt_ref, x2_ref, info_ref, gain_ref, y_ref, o_ref,
                ybuf_a, ybuf_b, tok1, tok2, sem, *, n_sl, tm, n_ch):
    i = pl.program_id(0)
    piece = MOE_CHUNK * n_sl

    def combine(ybuf, half):
        def unsort(t, carry):
            dst = pl.ds(pl.multiple_of(t * n_sl, n_sl), n_sl)
            tok1[dst, :] = ybuf[pl.ds(pl.multiple_of(p1_ref[half * tm + t] * n_sl, n_sl), n_sl), :]
            tok2[dst, :] = ybuf[pl.ds(pl.multiple_of(p2_ref[half * tm + t] * n_sl, n_sl), n_sl), :]
            return carry

        lax.fori_loop(0, tm, unsort, 0, unroll=8)
        rows = pl.ds(half * tm, tm)
        info = info_ref[rows, :]
        moe = (_slab_rows(tok1, tm, n_sl) * info[:, INFO_GATE:INFO_GATE + 1]
               + _slab_rows(tok2, tm, n_sl) * info[:, INFO_GATE + 1:INFO_GATE + 2])
        o_ref[rows, :] = _rms(x2_ref[rows, :] + moe, gain_ref[...])

    @pl.when(i == 0)
    def _():
        _start_rows(src_ref, 0, n_ch, y_ref, ybuf_a, sem.at[0], piece)

    _start_rows(src_ref, _index_stride(n_ch), n_ch, y_ref, ybuf_b, sem.at[1], piece)
    _wait_rows(n_ch, y_ref, ybuf_a, sem.at[0], piece)
    combine(ybuf_a, 0)
    _start_rows(nxt_ref, 0, n_ch, y_ref, ybuf_a, sem.at[0], piece)
    _wait_rows(n_ch, y_ref, ybuf_b, sem.at[1], piece)
    combine(ybuf_b, 1)

    @pl.when(i == pl.num_programs(0) - 1)
    def _():
        _wait_rows(n_ch, y_ref, ybuf_a, sem.at[0], piece)


def _final(lpos1, lpos2, y_src, x2, info, gain, y_rows, tm, cap, n_sl):
    T, D = x2.shape
    assert T % (2 * tm) == 0
    steps = T // (2 * tm)
    n_ch = cap // MOE_CHUNK
    stride = _index_stride(n_ch)
    y_src = _pad_lists(y_src, n_ch)
    cur = lambda i: (i,)
    nxt = lambda i: (jnp.minimum(2 * i + 2, 2 * steps - 1),)
    ybuf = pltpu.VMEM((cap * n_sl, LANE), F32)
    tbuf = pltpu.VMEM((tm * n_sl, LANE), F32)
    return pl.pallas_call(
        functools.partial(_final_body, n_sl=n_sl, tm=tm, n_ch=n_ch),
        grid=(steps,),
        in_specs=[
            pl.BlockSpec((2 * tm,), cur, memory_space=pltpu.SMEM),
            pl.BlockSpec((2 * tm,), cur, memory_space=pltpu.SMEM),
            pl.BlockSpec((2 * stride,), cur, memory_space=pltpu.SMEM),
            pl.BlockSpec((stride,), nxt, memory_space=pltpu.SMEM),
            pl.BlockSpec((2 * tm, D), lambda i: (i, 0)),
            pl.BlockSpec((2 * tm, LANE), lambda i: (i, 0)),
            pl.BlockSpec((1, D), lambda i: (0, 0)),
            pl.BlockSpec(memory_space=pl.ANY),
        ],
        out_specs=pl.BlockSpec((2 * tm, D), lambda i: (i, 0)),
        out_shape=jax.ShapeDtypeStruct((T, D), F32),
        scratch_shapes=[ybuf, ybuf, tbuf, tbuf, pltpu.SemaphoreType.DMA((2,))],
        compiler_params=_params(1),
        name="final",
    )(lpos1, lpos2, y_src, y_src, x2, info, gain, y_rows)


def _tile(n, pref):
    return pref if n % pref == 0 else n


def _layer(x, p):
    B, S, D = x.shape
    T = B * S
    n_sl = D // LANE
    G = p["pool_w"].shape[0]
    E = p["w1"].shape[0]
    n_groups = p["n_groups"]
    tm = _tile(S, 512)
    n_tiles = T // tm

    u = _inproj(x, p["norm_mix"], p["w_in"], tm)
    mixp = _pool(u, p["pool_w"], p["pool_scale"])
    mixh = _hgrn(u, p["lb_f"], p["lb_b"], p["hg_gain"], G)
    x2, xn_rows, info, tab = _outproj(mixp, mixh, x, p["w_out"], p["norm_ffn"], p["wr_split"],
                                      p["rbias"], tm, n_groups, E // n_groups)

    blk = EXPERT_ROWS
    ch = MOE_CHUNK
    cap = TOP_K * tm + E * ch
    cap_ch = cap // ch
    n_blocks = -(-(T * TOP_K + n_tiles * E * (ch - 1)) // blk) + E + 1
    n_blocks += n_blocks % 2
    n_slot_chunks = n_blocks * blk // ch
    tab = tab.reshape(n_tiles, SUBLANE, LANE)[:, :, n_groups:n_groups + E].astype(jnp.int32)
    count, before, lstart = tab[:, TAB_COUNT], tab[:, TAB_BEFORE], tab[:, TAB_LSTART]
    total = before[-1] + count[-1]
    padded = (total + blk - 1) // blk * blk
    padded = padded.at[E - 1].add(blk * ((jnp.sum(padded) // blk) % 2))
    pend = jnp.cumsum(padded)
    pstart = pend - padded
    first_slot = jnp.arange(n_blocks, dtype=jnp.int32) * blk
    block_expert = jnp.minimum(jnp.sum(pend[None, :] <= first_slot[:, None], axis=1), E - 1).astype(jnp.int32)
    n_used = (pend[-1:] // blk).astype(jnp.int32)

    slot_chunk = (pstart[None, :] + before) // ch
    local_chunk = lstart // ch
    n_local = (lstart[:, -1] + count[:, -1]) // ch
    off = slot_chunk - local_chunk
    step = off - jnp.concatenate([jnp.zeros((n_tiles, 1), jnp.int32), off[:, :-1]], axis=1)
    steps = jnp.zeros((n_tiles, cap_ch + 1), jnp.int32).at[jnp.arange(n_tiles)[:, None], local_chunk].add(step)
    chunk_slot = jnp.arange(cap_ch, dtype=jnp.int32)[None, :] + jnp.cumsum(steps, axis=1)[:, :cap_ch]
    chunk_slot = jnp.clip(chunk_slot, 0, n_slot_chunks - 1).reshape(-1)
    k = jnp.arange(2 * blk // ch, dtype=jnp.int32)[None, :]
    pad_chunks = ((pstart + total) // ch)[:, None] + k
    pad_chunks = jnp.where(pad_chunks < (pend // ch)[:, None], pad_chunks, -1).reshape(-1)
    pad_chunks = jnp.concatenate([pad_chunks, pend[-1:] // ch])

    info2 = info.reshape(T, LANE)
    lpos = info2[:, INFO_LPOS:INFO_LPOS + TOP_K].astype(jnp.int32)
    xs_rows = _dispatch(n_local, pad_chunks, lpos[:, 0], lpos[:, 1], chunk_slot, xn_rows, n_slot_chunks, tm, cap, n_sl)
    y_rows = _experts(block_expert, n_used, xs_rows, p["w1"], p["w3"], p["w2"], n_sl)
    out = _final(lpos[:, 0], lpos[:, 1], chunk_slot, x2.reshape(T, D), info2, p["norm_final"], y_rows, tm, cap, n_sl)
    return out.reshape(B, S, D)


def kernel(x_prompt, x_sample, w_in, w_out, pool_w, pool_scale, hg_lb_fwd, hg_lb_bwd, hg_norm_gain, norm_mix, norm_ffn, router_group_w, router_group_b, router_expert_w, router_expert_b, expert_w1, expert_w3, expert_w2, norm_final):
    depth = w_in.shape[0]
    assert depth == 1, "the final norm is fused into the last layer's combine kernel; one layer supported"
    D = w_in.shape[1]
    hg_width = hg_lb_fwd.shape[1]
    dv = hg_norm_gain.shape[1]
    pg = pool_w.shape[2]
    assert dv == LANE and pg == LANE and D % LANE == 0
    H = hg_width // dv
    n_groups = router_group_w.shape[-1]
    E = router_expert_w.shape[-1]
    assert n_groups + E <= LANE

    lb_f = jnp.cumsum(jax.nn.softmax(hg_lb_fwd.astype(F32), axis=0), axis=0)
    lb_b = jnp.cumsum(jax.nn.softmax(hg_lb_bwd.astype(F32), axis=0), axis=0)
    l = 0
    wr = jnp.concatenate([router_group_w[l], router_expert_w[l]], axis=1).astype(F32)
    wr = jnp.pad(wr, ((0, 0), (0, LANE - wr.shape[1])))
    wr_hi = wr.astype(BF16)
    rbias = jnp.concatenate([router_group_b[l], router_expert_b[l]]).astype(F32)
    p = dict(
        n_groups=n_groups,
        w_in=w_in[l].astype(BF16), w_out=w_out[l].astype(BF16),
        pool_w=pool_w[l], pool_scale=pool_scale[l],
        lb_f=lb_f[l].reshape(H, 1, dv), lb_b=lb_b[l].reshape(H, 1, dv), hg_gain=hg_norm_gain[l].reshape(1, dv).astype(F32),
        norm_mix=norm_mix[l].reshape(1, D).astype(F32), norm_ffn=norm_ffn[l].reshape(1, D).astype(F32),
        norm_final=norm_final.reshape(1, D).astype(F32),
        wr_split=jnp.concatenate([wr_hi, (wr - wr_hi.astype(F32)).astype(BF16)], axis=1),
        rbias=jnp.pad(rbias, (0, LANE - rbias.shape[0])).reshape(1, LANE),
        w1=expert_w1[l].astype(BF16), w3=expert_w3[l].astype(BF16), w2=expert_w2[l].astype(BF16),
    )
    return (_layer(x_prompt, p), _layer(x_sample, p))
```

```python
import functools

import jax
import jax.numpy as jnp
import numpy as np
from jax import lax
from jax.experimental import pallas as pl
from jax.experimental.pallas import tpu as pltpu

F32 = jnp.float32
BF16 = jnp.bfloat16

EPS = 1e-6
POOL_WINDOWS = (2, 4, 8, 16)
TOP_K = 2

LANE = 128
SUBLANE = 8
VMEM_LIMIT = 56 * 1024 * 1024

HG_CHUNK = 32
HG_GROUP = 128
HG_UNROLL = 8
POOL_ROWS = 128
POOL_UNROLL = 4
EXPERT_ROWS = 256


def _params(n_axes):
    return pltpu.CompilerParams(dimension_semantics=("arbitrary",) * n_axes, vmem_limit_bytes=VMEM_LIMIT)


def _rms(x, gain):
    return x * lax.rsqrt(jnp.mean(x * x, axis=-1, keepdims=True) + EPS) * gain


def _inproj_body(x_ref, gain_ref, w_ref, u_ref, *, nc):
    n = _rms(x_ref[0], gain_ref[...]).astype(BF16)
    per = nc // LANE
    for c in range(w_ref.shape[1] // nc):
        r = jnp.dot(n, w_ref[:, c * nc:(c + 1) * nc], preferred_element_type=F32)
        for j in range(per):
            u_ref[0, c * per + j] = r[:, j * LANE:(j + 1) * LANE].astype(BF16)


def _inproj(x, gain, w, tm):
    B, S, D = x.shape
    cols = w.shape[1]
    return pl.pallas_call(
        functools.partial(_inproj_body, nc=4 * LANE),
        grid=(B, S // tm),
        in_specs=[
            pl.BlockSpec((1, tm, D), lambda b, i: (b, i, 0)),
            pl.BlockSpec((1, D), lambda b, i: (0, 0)),
            pl.BlockSpec((D, cols), lambda b, i: (0, 0)),
        ],
        out_specs=pl.BlockSpec((1, cols // LANE, tm, LANE), lambda b, i: (b, 0, i, 0)),
        out_shape=jax.ShapeDtypeStruct((B, cols // LANE, S, LANE), BF16),
        compiler_params=_params(2),
        name="inproj",
    )(x, gain, w)


def _band_matrices(rows):
    t = np.arange(rows)[:, None]
    s = np.arange(rows)[None, :]
    out = np.zeros((len(POOL_WINDOWS), 3, rows, rows), np.float32)
    for gi, w in enumerate(POOL_WINDOWS):
        for k, shift in enumerate((-rows, 0, rows)):
            pos = s + shift
            out[gi, k] = (pos >= t - w // 2) & (pos < t + w // 2)
    return out


def _pool_body(half_ref, u_ref, band_ref, pw_ref, sc_ref, o_ref, *, seq, rows, unroll):
    nt = seq // rows
    h = half_ref[pl.program_id(1)]

    def window_sum(i):
        r0 = pl.multiple_of(i * rows, rows)
        rp = pl.multiple_of(jnp.maximum(i - 1, 0) * rows, rows)
        rn = pl.multiple_of(jnp.minimum(i + 1, nt - 1) * rows, rows)
        xc = u_ref[0, 0, pl.ds(r0, rows), :]
        s = jnp.dot(band_ref[0, 1], xc, preferred_element_type=F32)
        sp = jnp.dot(band_ref[0, 0], u_ref[0, 0, pl.ds(rp, rows), :], preferred_element_type=F32)
        sn = jnp.dot(band_ref[0, 2], u_ref[0, 0, pl.ds(rn, rows), :], preferred_element_type=F32)
        return r0, xc, s + jnp.where(i > 0, sp, 0.0) + jnp.where(i < nt - 1, sn, 0.0)

    def pooled(r0, xc, s):
        t = r0 + lax.broadcasted_iota(jnp.int32, (rows, LANE), 0)
        cnt = (jnp.minimum(t + h, seq) - jnp.maximum(t - h, 0)).astype(F32)
        return (s / cnt - xc.astype(F32)).astype(BF16)

    def tiles(j, carry):
        sums = [window_sum(j * unroll + k) for k in range(unroll)]
        pools = [pooled(*a) for a in sums]
        ys = [jnp.dot(pv, pw_ref[0], preferred_element_type=F32) * sc_ref[0] for pv in pools]
        for (r0, _, _), y in zip(sums, ys):
            o_ref[0, 0, pl.ds(r0, rows), :] = y.astype(BF16)
        return carry

    lax.fori_loop(0, nt // unroll, tiles, 0)


def _pool(u, pool_w, pool_scale):
    B, _, S, _ = u.shape
    G = pool_w.shape[0]
    rows = min(POOL_ROWS, S)
    band = jnp.asarray(_band_matrices(rows), BF16)
    halves = jnp.asarray([w // 2 for w in POOL_WINDOWS], jnp.int32)
    grid_spec = pltpu.PrefetchScalarGridSpec(
        num_scalar_prefetch=1,
        grid=(B, G),
        in_specs=[
            pl.BlockSpec((1, 1, S, LANE), lambda b, g, h: (b, g, 0, 0)),
            pl.BlockSpec((1, 3, rows, rows), lambda b, g, h: (g, 0, 0, 0)),
            pl.BlockSpec((1, LANE, LANE), lambda b, g, h: (g, 0, 0)),
            pl.BlockSpec((1, 1, LANE), lambda b, g, h: (g, 0, 0)),
        ],
        out_specs=pl.BlockSpec((1, 1, S, LANE), lambda b, g, h: (b, g, 0, 0)),
    )
    return pl.pallas_call(
        functools.partial(_pool_body, seq=S, rows=rows, unroll=POOL_UNROLL if (S // rows) % POOL_UNROLL == 0 else 1),
        grid_spec=grid_spec,
        out_shape=jax.ShapeDtypeStruct((B, G, S, LANE), BF16),
        compiler_params=_params(2),
        name="pool",
    )(halves, u, band, pool_w.astype(BF16), pool_scale.reshape(G, 1, LANE).astype(F32))


def _hg_gates(q_ref, f_ref, v_ref, r0, lb):
    rows = pl.ds(r0, HG_GROUP)
    q = q_ref[0, 0, rows, :].astype(F32)
    q = q * jax.nn.sigmoid(q)
    f = lb + (1.0 - lb) * jax.nn.sigmoid(f_ref[0, 0, rows, :].astype(F32))
    g = jnp.log(f)
    g_hi = g.astype(BF16)
    g_lo = (g - g_hi.astype(F32)).astype(BF16)
    return q, 1.0 - f, v_ref[0, 0, rows, :], g_hi, g_lo


def _block_rows(rows):
    return jnp.concatenate([jnp.broadcast_to(r, (HG_CHUNK, LANE)) for r in rows], axis=0)


def _hg_levels(reverse):
    assert HG_GROUP == 4 * HG_CHUNK
    C = HG_CHUNK
    if reverse:
        mid = [c * C + C // 2 for c in range(4)]
        level1 = [(0, 1, C), (2, 3, 3 * C)]
        level2 = [((0, 1), (2, 3), 2 * C)]
        end = 0
    else:
        mid = [c * C + C // 2 - 1 for c in range(4)]
        level1 = [(1, 0, C - 1), (3, 2, 3 * C - 1)]
        level2 = [((2, 3), (0, 1), 2 * C - 1)]
        end = HG_GROUP - 1
    return mid, level1, level2, end


def _hg_decays(q, k, b, *, reverse):
    mid_rows, level1, level2, end_row = _hg_levels(reverse)
    at = lambda r: b[r][None]
    mid = [at(r) for r in mid_rows]
    end = at(end_row)
    b_mid = _block_rows(mid)
    q_mid = q * jnp.exp(b - b_mid)
    k_mid = k * jnp.exp(b_mid - b)
    zero = jnp.zeros((1, LANE), F32)
    q1, k1, q2, k2 = ([zero] * 4 for _ in range(4))
    for rb, cb, ref in level1:
        q1[rb] = jnp.exp(mid[rb] - at(ref))
        k1[cb] = jnp.exp(at(ref) - mid[cb])
    for rbs, cbs, ref in level2:
        for rb in rbs:
            q2[rb] = jnp.exp(mid[rb] - at(ref))
        for cb in cbs:
            k2[cb] = jnp.exp(at(ref) - mid[cb])
    scale = lambda x, rows: (x * _block_rows(rows)).astype(BF16)
    return dict(
        q_dec=scale(q_mid, [jnp.exp(m) for m in mid]), k_st=scale(k_mid, [jnp.exp(end - m) for m in mid]),
        q0=q_mid.astype(BF16), k0=k_mid.astype(BF16), q1=scale(q_mid, q1), k1=scale(k_mid, k1),
        q2=scale(q_mid, q2), k2=scale(k_mid, k2), dec=jnp.exp(end))


def _hg_masks(reverse):
    C = HG_CHUNK
    R = HG_GROUP
    ri = lax.broadcasted_iota(jnp.int32, (R, R), 0)
    ci = lax.broadcasted_iota(jnp.int32, (R, R), 1)
    rb, cb = ri // C, ci // C
    _, level1, level2, _ = _hg_levels(reverse)
    m0 = (rb == cb) & ((ci >= ri) if reverse else (ci <= ri))
    m1 = functools.reduce(lambda a, e: a | ((rb == e[0]) & (cb == e[1])), level1, jnp.zeros((R, R), jnp.bool_))
    (rbs, cbs, _), = level2
    m2 = ((rb == rbs[0]) | (rb == rbs[1])) & ((cb == cbs[0]) | (cb == cbs[1]))
    return m0, m1, m2


def _hgrn_body(q_ref, ff_ref, fb_ref, v_ref, g_ref, lbf_ref, lbb_ref, gain_ref, tri_ref,
               o_ref, of_scr, ob_scr, *, seq, rows, unroll):
    R = HG_GROUP
    n_groups = seq // R
    lb = (lbf_ref[0], lbb_ref[0])
    f_refs = (ff_ref, fb_ref)
    scr = (of_scr, ob_scr)
    masks = (_hg_masks(False), _hg_masks(True))
    contract_last = (((1,), (1,)), ((), ()))
    contract_first = (((0,), (0,)), ((), ()))

    def step(j, carry):
        streams = []
        for d in range(2):
            for i in range(unroll):
                gi = j * unroll + i
                streams.append((d, pl.multiple_of((gi if d == 0 else n_groups - 1 - gi) * R, R)))
        gates = [_hg_gates(q_ref, f_refs[d], v_ref, r0, lb[d]) for d, r0 in streams]
        cums = [jnp.dot(tri_ref[d], g_hi, preferred_element_type=F32)
                + jnp.dot(tri_ref[d], g_lo, preferred_element_type=F32)
                for (d, _), (_, _, _, g_hi, g_lo) in zip(streams, gates)]
        ops = [_hg_decays(q, k, b, reverse=(d == 1)) for (d, _), (q, k, _, _, _), b in zip(streams, gates, cums)]
        score = lambda a, b: lax.dot_general(a, b, contract_last, preferred_element_type=F32)
        levels = [(score(o["q0"], o["k0"]), score(o["q1"], o["k1"]), score(o["q2"], o["k2"])) for o in ops]
        kvs = [lax.dot_general(v, o["k_st"], contract_first, preferred_element_type=F32)
               for (_, _, v, _, _), o in zip(gates, ops)]
        intra = []
        for (d, _), (s0, s1, s2), (_, _, v, _, _) in zip(streams, levels, gates):
            m0, m1, m2 = masks[d]
            s = jnp.where(m0, s0, jnp.where(m1, s1, jnp.where(m2, s2, 0.0)))
            intra.append(jnp.dot(s.astype(BF16), v, preferred_element_type=F32))
        states = list(carry)
        for n, (d, r0) in enumerate(streams):
            inter = lax.dot_general(ops[n]["q_dec"], states[d].astype(BF16), contract_last,
                                    preferred_element_type=F32)
            states[d] = states[d] * ops[n]["dec"] + kvs[n]
            scr[d][pl.ds(r0, R), :] = intra[n] + inter
        return tuple(states)

    zero = jnp.zeros((LANE, LANE), F32)
    lax.fori_loop(0, n_groups // unroll, step, (zero, zero))

    def finish(i, carry):
        r = pl.ds(pl.multiple_of(i * rows, rows), rows)
        o = of_scr[r, :] + ob_scr[r, :]
        o = _rms(o, gain_ref[...])
        gate = g_ref[0, 0, r, :].astype(F32)
        o_ref[0, 0, r, :] = (o * (gate * jax.nn.sigmoid(gate))).astype(BF16)
        return carry

    lax.fori_loop(0, seq // rows, finish, 0, unroll=4 if (seq // rows) % 4 == 0 else 1)


def _hgrn(u, lb_f, lb_b, gain, n_pool):
    B, _, S, _ = u.shape
    H = lb_f.shape[0]
    C = HG_CHUNK
    R = HG_GROUP
    assert S % R == 0
    low = np.tril(np.ones((R, R), np.float32))
    tri = jnp.asarray(np.stack([low, low.T]), BF16)
    rows = min(256, S)
    unroll = HG_UNROLL if (S // R) % HG_UNROLL == 0 else 1

    def slab(k):
        return pl.BlockSpec((1, 1, S, LANE), lambda b, h, k=k: (b, n_pool + k * H + h, 0, 0))

    head_vec = pl.BlockSpec((1, 1, LANE), lambda b, h: (h, 0, 0))
    return pl.pallas_call(
        functools.partial(_hgrn_body, seq=S, rows=rows, unroll=unroll),
        grid=(B, H),
        in_specs=[slab(0), slab(1), slab(2), slab(3), slab(4), head_vec, head_vec,
                  pl.BlockSpec((1, LANE), lambda b, h: (0, 0)),
                  pl.BlockSpec((2, R, R), lambda b, h: (0, 0, 0))],
        out_specs=pl.BlockSpec((1, 1, S, LANE), lambda b, h: (b, h, 0, 0)),
        out_shape=jax.ShapeDtypeStruct((B, H, S, LANE), BF16),
        scratch_shapes=[pltpu.VMEM((S, LANE), F32), pltpu.VMEM((S, LANE), F32)],
        compiler_params=_params(2),
        name="hgrn",
    )(u, u, u, u, u, lb_f, lb_b, gain, tri)


INFO_GATE = 0
INFO_LPOS = 2

TAB_COUNT = 0
TAB_BEFORE = 1
TAB_LSTART = 2

MOE_CHUNK = 8


def _outproj_body(mp_ref, mh_ref, x_ref, wo_ref, gain_ref, wr_ref, rb_ref, ls_ref, us_ref,
                  x2_ref, xn_ref, info_ref, tab_ref, before_ref, *, n_groups, epg):
    tm = x_ref.shape[1]

    @pl.when((pl.program_id(0) == 0) & (pl.program_id(1) == 0))
    def _():
        before_ref[...] = jnp.zeros_like(before_ref)

    mix = jnp.concatenate([mp_ref[0, j] for j in range(mp_ref.shape[1])]
                          + [mh_ref[0, j] for j in range(mh_ref.shape[1])], axis=-1)
    x2 = x_ref[0] + jnp.dot(mix, wo_ref[...], preferred_element_type=F32)
    x2_ref[0] = x2
    xn = _rms(x2, gain_ref[...])
    for j in range(xn.shape[1] // LANE):
        xn_ref[pl.ds(j, tm, stride=xn.shape[1] // LANE), :] = xn[:, j * LANE:(j + 1) * LANE]

    xh = xn.astype(BF16)
    xl = (xn - xh.astype(F32)).astype(BF16)
    hh_hl = jnp.dot(xh, wr_ref[...], preferred_element_type=F32)
    logits = (hh_hl[:, :LANE] + hh_hl[:, LANE:]
              + jnp.dot(xl, wr_ref[:, :LANE], preferred_element_type=F32)) + rb_ref[...]

    lane = lax.broadcasted_iota(jnp.int32, (tm, LANE), 1)
    neg = jnp.float32(-jnp.inf)
    big = jnp.int32(LANE)

    def top(vals):
        m = jnp.max(vals, axis=-1, keepdims=True)
        return m, jnp.min(jnp.where(vals == m, lane, big), axis=-1, keepdims=True)

    glog = jnp.where(lane < n_groups, logits, neg)
    gmax, grp = top(glog)
    grp_prob = 1.0 / jnp.sum(jnp.exp(glog - gmax), axis=-1, keepdims=True)
    e_lo = n_groups + grp * epg
    elog = jnp.where((lane >= e_lo) & (lane < e_lo + epg), logits, neg)
    v1, i1 = top(elog)
    v2, i2 = top(jnp.where(lane == i1, neg, elog))
    e21 = jnp.exp(v2 - v1)
    gate1 = grp_prob / (1.0 + e21)
    gate2 = grp_prob * e21 / (1.0 + e21)

    hot1 = lane == i1
    hot2 = lane == i2
    onehot = jnp.where(hot1 | hot2, 1.0, 0.0)
    earlier = jnp.dot(ls_ref[...], onehot.astype(BF16), preferred_element_type=F32)
    chunks = jnp.ceil(jnp.sum(onehot, axis=0, keepdims=True) * (1.0 / MOE_CHUNK))
    chunks8 = jnp.broadcast_to(chunks, (SUBLANE, LANE))
    lstart = jnp.dot(chunks8.astype(BF16), us_ref[...], preferred_element_type=F32) * MOE_CHUNK
    pos = earlier + lstart[0:1]
    lpos1 = jnp.sum(jnp.where(hot1, pos, 0.0), axis=-1, keepdims=True)
    lpos2 = jnp.sum(jnp.where(hot2, pos, 0.0), axis=-1, keepdims=True)

    count = chunks8 * MOE_CHUNK
    row = lax.broadcasted_iota(jnp.int32, (SUBLANE, LANE), 0)
    tab_ref[...] = jnp.where(row == TAB_COUNT, count,
                             jnp.where(row == TAB_BEFORE, before_ref[...],
                                       jnp.where(row == TAB_LSTART, lstart, 0.0)))
    before_ref[...] = before_ref[...] + count

    info = jnp.zeros((tm, LANE), F32)
    for k, col in ((INFO_GATE, gate1), (INFO_GATE + 1, gate2), (INFO_LPOS, lpos1), (INFO_LPOS + 1, lpos2)):
        info = jnp.where(lane == k, col, info)
    info_ref[0] = info


def _outproj(mixp, mixh, x, w_out, gain, wr_split, rbias, tm, n_groups, epg):
    B, S, D = x.shape
    n_sl = D // LANE
    lstrict = jnp.asarray(np.tril(np.ones((tm, tm), np.float32), -1), BF16)
    ustrict = jnp.asarray(np.triu(np.ones((LANE, LANE), np.float32), 1), BF16)
    const = lambda b, i: (0, 0)
    tile = lambda b, i: (b * (S // tm) + i, 0)
    return pl.pallas_call(
        functools.partial(_outproj_body, n_groups=n_groups, epg=epg),
        grid=(B, S // tm),
        in_specs=[
            pl.BlockSpec((1, mixp.shape[1], tm, LANE), lambda b, i: (b, 0, i, 0)),
            pl.BlockSpec((1, mixh.shape[1], tm, LANE), lambda b, i: (b, 0, i, 0)),
            pl.BlockSpec((1, tm, D), lambda b, i: (b, i, 0)),
            pl.BlockSpec(w_out.shape, const),
            pl.BlockSpec((1, D), const),
            pl.BlockSpec((D, 2 * LANE), const),
            pl.BlockSpec((1, LANE), const),
            pl.BlockSpec((tm, tm), const),
            pl.BlockSpec((LANE, LANE), const),
        ],
        out_specs=[
            pl.BlockSpec((1, tm, D), lambda b, i: (b, i, 0)),
            pl.BlockSpec((tm * n_sl, LANE), tile),
            pl.BlockSpec((1, tm, LANE), lambda b, i: (b, i, 0)),
            pl.BlockSpec((SUBLANE, LANE), tile),
        ],
        out_shape=[
            jax.ShapeDtypeStruct((B, S, D), F32),
            jax.ShapeDtypeStruct((B * S * n_sl, LANE), F32),
            jax.ShapeDtypeStruct((B, S, LANE), F32),
            jax.ShapeDtypeStruct((B * (S // tm) * SUBLANE, LANE), F32),
        ],
        scratch_shapes=[pltpu.VMEM((SUBLANE, LANE), F32)],
        compiler_params=_params(2),
        name="outproj",
    )(mixp, mixh, x, w_out, gain, wr_split, rbias, lstrict, ustrict)


def _dispatch_body(nloc_ref, pad_ref, p1_ref, p2_ref, dst_ref, xn_ref, xs_ref, buf_a, buf_b, zbuf, sem,
                   *, tm, n_sl, n_ch, n_fix):
    i = pl.program_id(0)
    last = pl.num_programs(0) - 1
    piece = MOE_CHUNK * n_sl
    stride = _index_stride(n_ch)

    def chunk_copy(buf, j, chunk, s):
        d = pl.multiple_of(chunk * piece, piece)
        return pltpu.make_async_copy(buf.at[pl.ds(j * piece, piece)], xs_ref.at[pl.ds(d, piece)], sem.at[s])

    def start(buf, half, s):
        n_loc = nloc_ref[2 * i + half]
        for j in range(n_ch):
            copy = chunk_copy(buf, j, dst_ref[half * stride + j], s)
            if j < n_fix:
                copy.start()
            else:
                pl.when(j < n_loc)(copy.start)

    def wait_chunks(buf, n, s):
        pltpu.make_async_copy(buf.at[pl.ds(0, n * piece)], xs_ref.at[pl.ds(0, n * piece)], sem.at[s]).wait()

    def wait(buf, tile, s):
        n_loc = nloc_ref[tile]
        wait_chunks(buf, n_fix, s)
        for j in range(n_fix, n_ch):
            pl.when(j < n_loc)(functools.partial(wait_chunks, buf, 1, s))

    def sort(buf, half):
        buf[...] = jnp.zeros_like(buf)

        def move(t, carry):
            row = xn_ref[pl.ds(pl.multiple_of((half * tm + t) * n_sl, n_sl), n_sl), :]
            buf[pl.ds(pl.multiple_of(p1_ref[half * tm + t] * n_sl, n_sl), n_sl), :] = row
            buf[pl.ds(pl.multiple_of(p2_ref[half * tm + t] * n_sl, n_sl), n_sl), :] = row
            return carry

        lax.fori_loop(0, tm, move, 0, unroll=8)

    @pl.when(i == 0)
    def _():
        zbuf[...] = jnp.zeros_like(zbuf)
        n_listed = pad_ref.shape[0] - 1
        n_chunks = xs_ref.shape[0] // piece

        def fill(k, carry):
            pl.when(pad_ref[k] >= 0)(chunk_copy(zbuf, 0, jnp.maximum(pad_ref[k], 0), 2).start)
            return carry

        def filled(k, carry):
            pl.when(pad_ref[k] >= 0)(functools.partial(wait_chunks, zbuf, 1, 2))
            return carry

        def fill_tail(c, carry):
            chunk_copy(zbuf, 0, c, 2).start()
            return carry

        def filled_tail(c, carry):
            wait_chunks(zbuf, 1, 2)
            return carry

        lax.fori_loop(0, n_listed, fill, 0)
        lax.fori_loop(pad_ref[n_listed], n_chunks, fill_tail, 0)
        lax.fori_loop(0, n_listed, filled, 0)
        lax.fori_loop(pad_ref[n_listed], n_chunks, filled_tail, 0)

    @pl.when(i > 0)
    def _():
        wait(buf_a, 2 * i - 2, 0)

    sort(buf_a, 0)
    start(buf_a, 0, 0)

    @pl.when(i > 0)
    def _():
        wait(buf_b, 2 * i - 1, 1)

    sort(buf_b, 1)
    start(buf_b, 1, 1)

    @pl.when(i == last)
    def _():
        wait(buf_a, 2 * i, 0)
        wait(buf_b, 2 * i + 1, 1)


def _dispatch(n_local, pad_chunks, lpos1, lpos2, chunk_dst, xn_rows, n_slot_chunks, tm, cap, n_sl):
    n_tiles = lpos1.shape[0] // tm
    assert n_tiles % 2 == 0
    n_ch = cap // MOE_CHUNK
    stride = _index_stride(n_ch)
    piece = MOE_CHUNK * n_sl
    grid_spec = pltpu.PrefetchScalarGridSpec(
        num_scalar_prefetch=2,
        grid=(n_tiles // 2,),
        in_specs=[
            pl.BlockSpec((2 * tm,), lambda i, nl, pc: (i,), memory_space=pltpu.SMEM),
            pl.BlockSpec((2 * tm,), lambda i, nl, pc: (i,), memory_space=pltpu.SMEM),
            pl.BlockSpec((2 * stride,), lambda i, nl, pc: (i,), memory_space=pltpu.SMEM),
            pl.BlockSpec((2 * tm * n_sl, LANE), lambda i, nl, pc: (i, 0)),
        ],
        out_specs=pl.BlockSpec(memory_space=pl.ANY),
        scratch_shapes=[pltpu.VMEM((cap * n_sl, LANE), F32), pltpu.VMEM((cap * n_sl, LANE), F32),
                        pltpu.VMEM((piece, LANE), F32), pltpu.SemaphoreType.DMA((3,))],
    )
    return pl.pallas_call(
        functools.partial(_dispatch_body, tm=tm, n_sl=n_sl, n_ch=n_ch, n_fix=TOP_K * tm // MOE_CHUNK),
        grid_spec=grid_spec,
        out_shape=jax.ShapeDtypeStruct((n_slot_chunks * piece, LANE), F32),
        compiler_params=_params(1),
        name="dispatch",
    )(n_local, pad_chunks, lpos1, lpos2, _pad_lists(chunk_dst, n_ch), xn_rows)


def _start_rows(idx_ref, first, n, src_ref, dst_ref, sem, rows_per):
    for r in range(n):
        s = pl.multiple_of(idx_ref[first + r] * rows_per, rows_per)
        pltpu.make_async_copy(src_ref.at[pl.ds(s, rows_per)], dst_ref.at[pl.ds(r * rows_per, rows_per)], sem).start()


def _wait_rows(n, src_ref, dst_ref, sem, rows_per):
    total = n * rows_per
    pltpu.make_async_copy(src_ref.at[pl.ds(0, total)], dst_ref.at[pl.ds(0, total)], sem).wait()


def _index_stride(n):
    return max(LANE, pl.next_power_of_2(n))


def _pad_lists(idx, n):
    return jnp.pad(idx.reshape(-1, n), ((0, 0), (0, _index_stride(n) - n))).reshape(-1)


def _slab_rows(ref, n, n_sl, first=0):
    return jnp.concatenate([ref[pl.ds(first + j, n, stride=n_sl), :] for j in range(n_sl)], axis=-1)


def _experts_body(bexp_ref, nused_ref, xs_ref, w1a_ref, w3a_ref, w2a_ref, w1b_ref, w3b_ref, w2b_ref, y_ref, *, n_sl):
    i = pl.program_id(0)
    blk = EXPERT_ROWS

    def mlp(w1_ref, w3_ref, w2_ref, half):
        first = half * blk * n_sl
        xb = _slab_rows(xs_ref, blk, n_sl, first).astype(BF16)
        a = jnp.dot(xb, w1_ref[0], preferred_element_type=F32)
        h = (a * jax.nn.sigmoid(a)) * jnp.dot(xb, w3_ref[0], preferred_element_type=F32)
        y = jnp.dot(h.astype(BF16), w2_ref[0], preferred_element_type=F32)
        for j in range(n_sl):
            y_ref[pl.ds(first + j, blk, stride=n_sl), :] = y[:, j * LANE:(j + 1) * LANE]

    @pl.when(2 * i < nused_ref[0])
    def _():
        mlp(w1a_ref, w3a_ref, w2a_ref, 0)
        mlp(w1b_ref, w3b_ref, w2b_ref, 1)

    @pl.when(2 * i >= nused_ref[0])
    def _():
        y_ref[...] = jnp.zeros_like(y_ref)


def _experts(block_expert, n_used, xs_rows, w1, w3, w2, n_sl):
    n_blocks = block_expert.shape[0]
    assert n_blocks % 2 == 0
    blk = EXPERT_ROWS
    _, D, DE = w1.shape

    def wspec(shape, half):
        return pl.BlockSpec(shape, lambda i, be, nu, half=half: (be[2 * i + half], 0, 0))

    grid_spec = pltpu.PrefetchScalarGridSpec(
        num_scalar_prefetch=2,
        grid=(n_blocks // 2,),
        in_specs=[
            pl.BlockSpec((2 * blk * n_sl, LANE), lambda i, be, nu: (jnp.minimum(i, nu[0] // 2 - 1), 0)),
            wspec((1, D, DE), 0), wspec((1, D, DE), 0), wspec((1, DE, D), 0),
            wspec((1, D, DE), 1), wspec((1, D, DE), 1), wspec((1, DE, D), 1),
        ],
        out_specs=pl.BlockSpec((2 * blk * n_sl, LANE), lambda i, be, nu: (i, 0)),
    )
    return pl.pallas_call(
        functools.partial(_experts_body, n_sl=n_sl),
        grid_spec=grid_spec,
        out_shape=jax.ShapeDtypeStruct((n_blocks * blk * n_sl, LANE), F32),
        compiler_params=_params(1),
        name="experts",
    )(block_expert, n_used, xs_rows, w1, w3, w2, w1, w3, w2)


def _final_body(p1_ref, p2_ref, src_ref, nxt_ref, x2_ref, info_ref, gain_ref, y_ref, o_ref,
                ybuf_a, ybuf_b, tok1, tok2, sem, *, n_sl, tm, n_ch):
    i = pl.program_id(0)
    piece = MOE_CHUNK * n_sl

    def combine(ybuf, half):
        def unsort(t, carry):
            dst = pl.ds(pl.multiple_of(t * n_sl, n_sl), n_sl)
            tok1[dst, :] = ybuf[pl.ds(pl.multiple_of(p1_ref[half * tm + t] * n_sl, n_sl), n_sl), :]
            tok2[dst, :] = ybuf[pl.ds(pl.multiple_of(p2_ref[half * tm + t] * n_sl, n_sl), n_sl), :]
            return carry

        lax.fori_loop(0, tm, unsort, 0, unroll=8)
        rows = pl.ds(half * tm, tm)
        info = info_ref[rows, :]
        moe = (_slab_rows(tok1, tm, n_sl) * info[:, INFO_GATE:INFO_GATE + 1]
               + _slab_rows(tok2, tm, n_sl) * info[:, INFO_GATE + 1:INFO_GATE + 2])
        o_ref[rows, :] = _rms(x2_ref[rows, :] + moe, gain_ref[...])

    @pl.when(i == 0)
    def _():
        _start_rows(src_ref, 0, n_ch, y_ref, ybuf_a, sem.at[0], piece)

    _start_rows(src_ref, _index_stride(n_ch), n_ch, y_ref, ybuf_b, sem.at[1], piece)
    _wait_rows(n_ch, y_ref, ybuf_a, sem.at[0], piece)
    combine(ybuf_a, 0)
    _start_rows(nxt_ref, 0, n_ch, y_ref, ybuf_a, sem.at[0], piece)
    _wait_rows(n_ch, y_ref, ybuf_b, sem.at[1], piece)
    combine(ybuf_b, 1)

    @pl.when(i == pl.num_programs(0) - 1)
    def _():
        _wait_rows(n_ch, y_ref, ybuf_a, sem.at[0], piece)


def _final(lpos1, lpos2, y_src, x2, info, gain, y_rows, tm, cap, n_sl):
    T, D = x2.shape
    assert T % (2 * tm) == 0
    steps = T // (2 * tm)
    n_ch = cap // MOE_CHUNK
    stride = _index_stride(n_ch)
    y_src = _pad_lists(y_src, n_ch)
    cur = lambda i: (i,)
    nxt = lambda i: (jnp.minimum(2 * i + 2, 2 * steps - 1),)
    ybuf = pltpu.VMEM((cap * n_sl, LANE), F32)
    tbuf = pltpu.VMEM((tm * n_sl, LANE), F32)
    return pl.pallas_call(
        functools.partial(_final_body, n_sl=n_sl, tm=tm, n_ch=n_ch),
        grid=(steps,),
        in_specs=[
            pl.BlockSpec((2 * tm,), cur, memory_space=pltpu.SMEM),
            pl.BlockSpec((2 * tm,), cur, memory_space=pltpu.SMEM),
            pl.BlockSpec((2 * stride,), cur, memory_space=pltpu.SMEM),
            pl.BlockSpec((stride,), nxt, memory_space=pltpu.SMEM),
            pl.BlockSpec((2 * tm, D), lambda i: (i, 0)),
            pl.BlockSpec((2 * tm, LANE), lambda i: (i, 0)),
            pl.BlockSpec((1, D), lambda i: (0, 0)),
            pl.BlockSpec(memory_space=pl.ANY),
        ],
        out_specs=pl.BlockSpec((2 * tm, D), lambda i: (i, 0)),
        out_shape=jax.ShapeDtypeStruct((T, D), F32),
        scratch_shapes=[ybuf, ybuf, tbuf, tbuf, pltpu.SemaphoreType.DMA((2,))],
        compiler_params=_params(1),
        name="final",
    )(lpos1, lpos2, y_src, y_src, x2, info, gain, y_rows)


def _tile(n, pref):
    return pref if n % pref == 0 else n


def _layer(x, p):
    B, S, D = x.shape
    T = B * S
    n_sl = D // LANE
    G = p["pool_w"].shape[0]
    E = p["w1"].shape[0]
    n_groups = p["n_groups"]
    tm = _tile(S, 512)
    n_tiles = T // tm

    u = _inproj(x, p["norm_mix"], p["w_in"], tm)
    mixp = _pool(u, p["pool_w"], p["pool_scale"])
    mixh = _hgrn(u, p["lb_f"], p["lb_b"], p["hg_gain"], G)
    x2, xn_rows, info, tab = _outproj(mixp, mixh, x, p["w_out"], p["norm_ffn"], p["wr_split"],
                                      p["rbias"], tm, n_groups, E // n_groups)

    blk = EXPERT_ROWS
    ch = MOE_CHUNK
    cap = TOP_K * tm + E * ch
    cap_ch = cap // ch
    n_blocks = -(-(T * TOP_K + n_tiles * E * (ch - 1)) // blk) + E + 1
    n_blocks += n_blocks % 2
    n_slot_chunks = n_blocks * blk // ch
    tab = tab.reshape(n_tiles, SUBLANE, LANE)[:, :, n_groups:n_groups + E].astype(jnp.int32)
    count, before, lstart = tab[:, TAB_COUNT], tab[:, TAB_BEFORE], tab[:, TAB_LSTART]
    total = before[-1] + count[-1]
    padded = (total + blk - 1) // blk * blk
    padded = padded.at[E - 1].add(blk * ((jnp.sum(padded) // blk) % 2))
    pend = jnp.cumsum(padded)
    pstart = pend - padded
    first_slot = jnp.arange(n_blocks, dtype=jnp.int32) * blk
    block_expert = jnp.minimum(jnp.sum(pend[None, :] <= first_slot[:, None], axis=1), E - 1).astype(jnp.int32)
    n_used = (pend[-1:] // blk).astype(jnp.int32)

    slot_chunk = (pstart[None, :] + before) // ch
    local_chunk = lstart // ch
    n_local = (lstart[:, -1] + count[:, -1]) // ch
    off = slot_chunk - local_chunk
    step = off - jnp.concatenate([jnp.zeros((n_tiles, 1), jnp.int32), off[:, :-1]], axis=1)
    steps = jnp.zeros((n_tiles, cap_ch + 1), jnp.int32).at[jnp.arange(n_tiles)[:, None], local_chunk].add(step)
    chunk_slot = jnp.arange(cap_ch, dtype=jnp.int32)[None, :] + jnp.cumsum(steps, axis=1)[:, :cap_ch]
    chunk_slot = jnp.clip(chunk_slot, 0, n_slot_chunks - 1).reshape(-1)
    k = jnp.arange(2 * blk // ch, dtype=jnp.int32)[None, :]
    pad_chunks = ((pstart + total) // ch)[:, None] + k
    pad_chunks = jnp.where(pad_chunks < (pend // ch)[:, None], pad_chunks, -1).reshape(-1)
    pad_chunks = jnp.concatenate([pad_chunks, pend[-1:] // ch])

    info2 = info.reshape(T, LANE)
    lpos = info2[:, INFO_LPOS:INFO_LPOS + TOP_K].astype(jnp.int32)
    xs_rows = _dispatch(n_local, pad_chunks, lpos[:, 0], lpos[:, 1], chunk_slot, xn_rows, n_slot_chunks, tm, cap, n_sl)
    y_rows = _experts(block_expert, n_used, xs_rows, p["w1"], p["w3"], p["w2"], n_sl)
    out = _final(lpos[:, 0], lpos[:, 1], chunk_slot, x2.reshape(T, D), info2, p["norm_final"], y_rows, tm, cap, n_sl)
    return out.reshape(B, S, D)


def kernel(x_prompt, x_sample, w_in, w_out, pool_w, pool_scale, hg_lb_fwd, hg_lb_bwd, hg_norm_gain, norm_mix, norm_ffn, router_group_w, router_group_b, router_expert_w, router_expert_b, expert_w1, expert_w3, expert_w2, norm_final):
    depth = w_in.shape[0]
    assert depth == 1, "the final norm is fused into the last layer's combine kernel; one layer supported"
    D = w_in.shape[1]
    hg_width = hg_lb_fwd.shape[1]
    dv = hg_norm_gain.shape[1]
    pg = pool_w.shape[2]
    assert dv == LANE and pg == LANE and D % LANE == 0
    H = hg_width // dv
    n_groups = router_group_w.shape[-1]
    E = router_expert_w.shape[-1]
    assert n_groups + E <= LANE

    lb_f = jnp.cumsum(jax.nn.softmax(hg_lb_fwd.astype(F32), axis=0), axis=0)
    lb_b = jnp.cumsum(jax.nn.softmax(hg_lb_bwd.astype(F32), axis=0), axis=0)
    l = 0
    wr = jnp.concatenate([router_group_w[l], router_expert_w[l]], axis=1).astype(F32)
    wr = jnp.pad(wr, ((0, 0), (0, LANE - wr.shape[1])))
    wr_hi = wr.astype(BF16)
    rbias = jnp.concatenate([router_group_b[l], router_expert_b[l]]).astype(F32)
    p = dict(
        n_groups=n_groups,
        w_in=w_in[l].astype(BF16), w_out=w_out[l].astype(BF16),
        pool_w=pool_w[l], pool_scale=pool_scale[l],
        lb_f=lb_f[l].reshape(H, 1, dv), lb_b=lb_b[l].reshape(H, 1, dv), hg_gain=hg_norm_gain[l].reshape(1, dv).astype(F32),
        norm_mix=norm_mix[l].reshape(1, D).astype(F32), norm_ffn=norm_ffn[l].reshape(1, D).astype(F32),
        norm_final=norm_final.reshape(1, D).astype(F32),
        wr_split=jnp.concatenate([wr_hi, (wr - wr_hi.astype(F32)).astype(BF16)], axis=1),
        rbias=jnp.pad(rbias, (0, LANE - rbias.shape[0])).reshape(1, LANE),
        w1=expert_w1[l].astype(BF16), w3=expert_w3[l].astype(BF16), w2=expert_w2[l].astype(BF16),
    )
    return (_layer(x_prompt, p), _layer(x_sample, p))
```

```python
import functools

import jax
import jax.numpy as jnp
import numpy as np
from jax import lax
from jax.experimental import pallas as pl
from jax.experimental.pallas import tpu as pltpu

F32 = jnp.float32
BF16 = jnp.bfloat16

EPS = 1e-6
POOL_WINDOWS = (2, 4, 8, 16)
TOP_K = 2

LANE = 128
SUBLANE = 8
VMEM_LIMIT = 56 * 1024 * 1024

HG_CHUNK = 32
HG_GROUP = 128
HG_UNROLL = 8
POOL_ROWS = 128
POOL_UNROLL = 4
EXPERT_ROWS = 256
EXPERT_STEP = 4


def _params(n_axes):
    return pltpu.CompilerParams(dimension_semantics=("arbitrary",) * n_axes, vmem_limit_bytes=VMEM_LIMIT)


def _rms(x, gain):
    return x * lax.rsqrt(jnp.mean(x * x, axis=-1, keepdims=True) + EPS) * gain


def _inproj_body(x_ref, gain_ref, w_ref, u_ref, *, nc):
    n = _rms(x_ref[0], gain_ref[...]).astype(BF16)
    per = nc // LANE
    for c in range(w_ref.shape[1] // nc):
        r = jnp.dot(n, w_ref[:, c * nc:(c + 1) * nc], preferred_element_type=F32)
        for j in range(per):
            u_ref[0, c * per + j] = r[:, j * LANE:(j + 1) * LANE].astype(BF16)


def _inproj(x, gain, w, tm):
    B, S, D = x.shape
    cols = w.shape[1]
    return pl.pallas_call(
        functools.partial(_inproj_body, nc=4 * LANE),
        grid=(B, S // tm),
        in_specs=[
            pl.BlockSpec((1, tm, D), lambda b, i: (b, i, 0)),
            pl.BlockSpec((1, D), lambda b, i: (0, 0)),
            pl.BlockSpec((D, cols), lambda b, i: (0, 0)),
        ],
        out_specs=pl.BlockSpec((1, cols // LANE, tm, LANE), lambda b, i: (b, 0, i, 0)),
        out_shape=jax.ShapeDtypeStruct((B, cols // LANE, S, LANE), BF16),
        compiler_params=_params(2),
        name="inproj",
    )(x, gain, w)


def _band_matrices(rows):
    t = np.arange(rows)[:, None]
    s = np.arange(rows)[None, :]
    out = np.zeros((len(POOL_WINDOWS), 3, rows, rows), np.float32)
    for gi, w in enumerate(POOL_WINDOWS):
        for k, shift in enumerate((-rows, 0, rows)):
            pos = s + shift
            out[gi, k] = (pos >= t - w // 2) & (pos < t + w // 2)
    return out


def _pool_body(half_ref, u_ref, band_ref, pw_ref, sc_ref, o_ref, *, seq, rows, unroll):
    nt = seq // rows
    h = half_ref[pl.program_id(1)]

    def window_sum(i):
        r0 = pl.multiple_of(i * rows, rows)
        rp = pl.multiple_of(jnp.maximum(i - 1, 0) * rows, rows)
        rn = pl.multiple_of(jnp.minimum(i + 1, nt - 1) * rows, rows)
        xc = u_ref[0, 0, pl.ds(r0, rows), :]
        s = jnp.dot(band_ref[0, 1], xc, preferred_element_type=F32)
        sp = jnp.dot(band_ref[0, 0], u_ref[0, 0, pl.ds(rp, rows), :], preferred_element_type=F32)
        sn = jnp.dot(band_ref[0, 2], u_ref[0, 0, pl.ds(rn, rows), :], preferred_element_type=F32)
        return r0, xc, s + jnp.where(i > 0, sp, 0.0) + jnp.where(i < nt - 1, sn, 0.0)

    def pooled(r0, xc, s):
        t = r0 + lax.broadcasted_iota(jnp.int32, (rows, LANE), 0)
        cnt = (jnp.minimum(t + h, seq) - jnp.maximum(t - h, 0)).astype(F32)
        return (s / cnt - xc.astype(F32)).astype(BF16)

    def tiles(j, carry):
        sums = [window_sum(j * unroll + k) for k in range(unroll)]
        pools = [pooled(*a) for a in sums]
        ys = [jnp.dot(pv, pw_ref[0], preferred_element_type=F32) * sc_ref[0] for pv in pools]
        for (r0, _, _), y in zip(sums, ys):
            o_ref[0, 0, pl.ds(r0, rows), :] = y.astype(BF16)
        return carry

    lax.fori_loop(0, nt // unroll, tiles, 0)


def _pool(u, pool_w, pool_scale):
    B, _, S, _ = u.shape
    G = pool_w.shape[0]
    rows = min(POOL_ROWS, S)
    band = jnp.asarray(_band_matrices(rows), BF16)
    halves = jnp.asarray([w // 2 for w in POOL_WINDOWS], jnp.int32)
    grid_spec = pltpu.PrefetchScalarGridSpec(
        num_scalar_prefetch=1,
        grid=(B, G),
        in_specs=[
            pl.BlockSpec((1, 1, S, LANE), lambda b, g, h: (b, g, 0, 0)),
            pl.BlockSpec((1, 3, rows, rows), lambda b, g, h: (g, 0, 0, 0)),
            pl.BlockSpec((1, LANE, LANE), lambda b, g, h: (g, 0, 0)),
            pl.BlockSpec((1, 1, LANE), lambda b, g, h: (g, 0, 0)),
        ],
        out_specs=pl.BlockSpec((1, 1, S, LANE), lambda b, g, h: (b, g, 0, 0)),
    )
    return pl.pallas_call(
        functools.partial(_pool_body, seq=S, rows=rows, unroll=POOL_UNROLL if (S // rows) % POOL_UNROLL == 0 else 1),
        grid_spec=grid_spec,
        out_shape=jax.ShapeDtypeStruct((B, G, S, LANE), BF16),
        compiler_params=_params(2),
        name="pool",
    )(halves, u, band, pool_w.astype(BF16), pool_scale.reshape(G, 1, LANE).astype(F32))


def _hg_gates(q_ref, f_ref, v_ref, r0, lb):
    rows = pl.ds(r0, HG_GROUP)
    q = q_ref[0, 0, rows, :].astype(F32)
    q = q * jax.nn.sigmoid(q)
    f = lb + (1.0 - lb) * jax.nn.sigmoid(f_ref[0, 0, rows, :].astype(F32))
    g = jnp.log(f)
    g_hi = g.astype(BF16)
    g_lo = (g - g_hi.astype(F32)).astype(BF16)
    return q, 1.0 - f, v_ref[0, 0, rows, :], g_hi, g_lo


def _block_rows(rows):
    return jnp.concatenate([jnp.broadcast_to(r, (HG_CHUNK, LANE)) for r in rows], axis=0)


def _hg_levels(reverse):
    assert HG_GROUP == 4 * HG_CHUNK
    C = HG_CHUNK
    if reverse:
        mid = [c * C + C // 2 for c in range(4)]
        level1 = [(0, 1, C), (2, 3, 3 * C)]
        level2 = [((0, 1), (2, 3), 2 * C)]
        end = 0
    else:
        mid = [c * C + C // 2 - 1 for c in range(4)]
        level1 = [(1, 0, C - 1), (3, 2, 3 * C - 1)]
        level2 = [((2, 3), (0, 1), 2 * C - 1)]
        end = HG_GROUP - 1
    return mid, level1, level2, end


def _hg_decays(q, k, b, *, reverse):
    mid_rows, level1, level2, end_row = _hg_levels(reverse)
    at = lambda r: b[r][None]
    mid = [at(r) for r in mid_rows]
    end = at(end_row)
    b_mid = _block_rows(mid)
    q_mid = q * jnp.exp(b - b_mid)
    k_mid = k * jnp.exp(b_mid - b)
    zero = jnp.zeros((1, LANE), F32)
    q1, k1, q2, k2 = ([zero] * 4 for _ in range(4))
    for rb, cb, ref in level1:
        q1[rb] = jnp.exp(mid[rb] - at(ref))
        k1[cb] = jnp.exp(at(ref) - mid[cb])
    for rbs, cbs, ref in level2:
        for rb in rbs:
            q2[rb] = jnp.exp(mid[rb] - at(ref))
        for cb in cbs:
            k2[cb] = jnp.exp(at(ref) - mid[cb])
    scale = lambda x, rows: (x * _block_rows(rows)).astype(BF16)
    return dict(
        q_dec=scale(q_mid, [jnp.exp(m) for m in mid]), k_st=scale(k_mid, [jnp.exp(end - m) for m in mid]),
        q0=q_mid.astype(BF16), k0=k_mid.astype(BF16), q1=scale(q_mid, q1), k1=scale(k_mid, k1),
        q2=scale(q_mid, q2), k2=scale(k_mid, k2), dec=jnp.exp(end))


def _hg_masks(reverse):
    C = HG_CHUNK
    R = HG_GROUP
    ri = lax.broadcasted_iota(jnp.int32, (R, R), 0)
    ci = lax.broadcasted_iota(jnp.int32, (R, R), 1)
    rb, cb = ri // C, ci // C
    _, level1, level2, _ = _hg_levels(reverse)
    m0 = (rb == cb) & ((ci >= ri) if reverse else (ci <= ri))
    m1 = functools.reduce(lambda a, e: a | ((rb == e[0]) & (cb == e[1])), level1, jnp.zeros((R, R), jnp.bool_))
    (rbs, cbs, _), = level2
    m2 = ((rb == rbs[0]) | (rb == rbs[1])) & ((cb == cbs[0]) | (cb == cbs[1]))
    return m0, m1, m2


def _hgrn_body(q_ref, ff_ref, fb_ref, v_ref, g_ref, lbf_ref, lbb_ref, gain_ref, tri_ref,
               o_ref, of_scr, ob_scr, *, seq, rows, unroll):
    R = HG_GROUP
    n_groups = seq // R
    lb = (lbf_ref[0], lbb_ref[0])
    f_refs = (ff_ref, fb_ref)
    scr = (of_scr, ob_scr)
    masks = (_hg_masks(False), _hg_masks(True))
    contract_last = (((1,), (1,)), ((), ()))
    contract_first = (((0,), (0,)), ((), ()))

    def step(j, carry):
        streams = []
        for d in range(2):
            for i in range(unroll):
                gi = j * unroll + i
                streams.append((d, pl.multiple_of((gi if d == 0 else n_groups - 1 - gi) * R, R)))
        gates = [_hg_gates(q_ref, f_refs[d], v_ref, r0, lb[d]) for d, r0 in streams]
        cums = [jnp.dot(tri_ref[d], g_hi, preferred_element_type=F32)
                + jnp.dot(tri_ref[d], g_lo, preferred_element_type=F32)
                for (d, _), (_, _, _, g_hi, g_lo) in zip(streams, gates)]
        ops = [_hg_decays(q, k, b, reverse=(d == 1)) for (d, _), (q, k, _, _, _), b in zip(streams, gates, cums)]
        score = lambda a, b: lax.dot_general(a, b, contract_last, preferred_element_type=F32)
        levels = [(score(o["q0"], o["k0"]), score(o["q1"], o["k1"]), score(o["q2"], o["k2"])) for o in ops]
        kvs = [lax.dot_general(v, o["k_st"], contract_first, preferred_element_type=F32)
               for (_, _, v, _, _), o in zip(gates, ops)]
        intra = []
        for (d, _), (s0, s1, s2), (_, _, v, _, _) in zip(streams, levels, gates):
            m0, m1, m2 = masks[d]
            s = jnp.where(m0, s0, jnp.where(m1, s1, jnp.where(m2, s2, 0.0)))
            intra.append(jnp.dot(s.astype(BF16), v, preferred_element_type=F32))
        states = list(carry)
        for n, (d, r0) in enumerate(streams):
            inter = lax.dot_general(ops[n]["q_dec"], states[d].astype(BF16), contract_last,
                                    preferred_element_type=F32)
            states[d] = states[d] * ops[n]["dec"] + kvs[n]
            scr[d][pl.ds(r0, R), :] = intra[n] + inter
        return tuple(states)

    zero = jnp.zeros((LANE, LANE), F32)
    lax.fori_loop(0, n_groups // unroll, step, (zero, zero))

    def finish(i, carry):
        r = pl.ds(pl.multiple_of(i * rows, rows), rows)
        o = of_scr[r, :] + ob_scr[r, :]
        o = _rms(o, gain_ref[...])
        gate = g_ref[0, 0, r, :].astype(F32)
        o_ref[0, 0, r, :] = (o * (gate * jax.nn.sigmoid(gate))).astype(BF16)
        return carry

    lax.fori_loop(0, seq // rows, finish, 0, unroll=4 if (seq // rows) % 4 == 0 else 1)


def _hgrn(u, lb_f, lb_b, gain, n_pool):
    B, _, S, _ = u.shape
    H = lb_f.shape[0]
    C = HG_CHUNK
    R = HG_GROUP
    assert S % R == 0
    low = np.tril(np.ones((R, R), np.float32))
    tri = jnp.asarray(np.stack([low, low.T]), BF16)
    rows = min(256, S)
    unroll = HG_UNROLL if (S // R) % HG_UNROLL == 0 else 1

    def slab(k):
        return pl.BlockSpec((1, 1, S, LANE), lambda b, h, k=k: (b, n_pool + k * H + h, 0, 0))

    head_vec = pl.BlockSpec((1, 1, LANE), lambda b, h: (h, 0, 0))
    return pl.pallas_call(
        functools.partial(_hgrn_body, seq=S, rows=rows, unroll=unroll),
        grid=(B, H),
        in_specs=[slab(0), slab(1), slab(2), slab(3), slab(4), head_vec, head_vec,
                  pl.BlockSpec((1, LANE), lambda b, h: (0, 0)),
                  pl.BlockSpec((2, R, R), lambda b, h: (0, 0, 0))],
        out_specs=pl.BlockSpec((1, 1, S, LANE), lambda b, h: (b, h, 0, 0)),
        out_shape=jax.ShapeDtypeStruct((B, H, S, LANE), BF16),
        scratch_shapes=[pltpu.VMEM((S, LANE), F32), pltpu.VMEM((S, LANE), F32)],
        compiler_params=_params(2),
        name="hgrn",
    )(u, u, u, u, u, lb_f, lb_b, gain, tri)


INFO_GATE = 0
INFO_LPOS = 2

TAB_COUNT = 0
TAB_BEFORE = 1
TAB_LSTART = 2

MOE_CHUNK = 8


def _outproj_body(mp_ref, mh_ref, x_ref, wo_ref, gain_ref, wr_ref, rb_ref, ls_ref, us_ref,
                  x2_ref, xn_ref, info_ref, tab_ref, before_ref, *, n_groups, epg):
    tm = x_ref.shape[1]

    @pl.when((pl.program_id(0) == 0) & (pl.program_id(1) == 0))
    def _():
        before_ref[...] = jnp.zeros_like(before_ref)

    mix = jnp.concatenate([mp_ref[0, j] for j in range(mp_ref.shape[1])]
                          + [mh_ref[0, j] for j in range(mh_ref.shape[1])], axis=-1)
    x2 = x_ref[0] + jnp.dot(mix, wo_ref[...], preferred_element_type=F32)
    x2_ref[0] = x2
    xn = _rms(x2, gain_ref[...])
    for j in range(xn.shape[1] // LANE):
        xn_ref[pl.ds(j, tm, stride=xn.shape[1] // LANE), :] = xn[:, j * LANE:(j + 1) * LANE]

    xh = xn.astype(BF16)
    xl = (xn - xh.astype(F32)).astype(BF16)
    hh_hl = jnp.dot(xh, wr_ref[...], preferred_element_type=F32)
    logits = (hh_hl[:, :LANE] + hh_hl[:, LANE:]
              + jnp.dot(xl, wr_ref[:, :LANE], preferred_element_type=F32)) + rb_ref[...]

    lane = lax.broadcasted_iota(jnp.int32, (tm, LANE), 1)
    neg = jnp.float32(-jnp.inf)
    big = jnp.int32(LANE)

    def top(vals):
        m = jnp.max(vals, axis=-1, keepdims=True)
        return m, jnp.min(jnp.where(vals == m, lane, big), axis=-1, keepdims=True)

    glog = jnp.where(lane < n_groups, logits, neg)
    gmax, grp = top(glog)
    grp_prob = 1.0 / jnp.sum(jnp.exp(glog - gmax), axis=-1, keepdims=True)
    e_lo = n_groups + grp * epg
    elog = jnp.where((lane >= e_lo) & (lane < e_lo + epg), logits, neg)
    v1, i1 = top(elog)
    v2, i2 = top(jnp.where(lane == i1, neg, elog))
    e21 = jnp.exp(v2 - v1)
    gate1 = grp_prob / (1.0 + e21)
    gate2 = grp_prob * e21 / (1.0 + e21)

    hot1 = lane == i1
    hot2 = lane == i2
    onehot = jnp.where(hot1 | hot2, 1.0, 0.0)
    earlier = jnp.dot(ls_ref[...], onehot.astype(BF16), preferred_element_type=F32)
    chunks = jnp.ceil(jnp.sum(onehot, axis=0, keepdims=True) * (1.0 / MOE_CHUNK))
    chunks8 = jnp.broadcast_to(chunks, (SUBLANE, LANE))
    lstart = jnp.dot(chunks8.astype(BF16), us_ref[...], preferred_element_type=F32) * MOE_CHUNK
    pos = earlier + lstart[0:1]
    lpos1 = jnp.sum(jnp.where(hot1, pos, 0.0), axis=-1, keepdims=True)
    lpos2 = jnp.sum(jnp.where(hot2, pos, 0.0), axis=-1, keepdims=True)

    count = chunks8 * MOE_CHUNK
    row = lax.broadcasted_iota(jnp.int32, (SUBLANE, LANE), 0)
    tab_ref[...] = jnp.where(row == TAB_COUNT, count,
                             jnp.where(row == TAB_BEFORE, before_ref[...],
                                       jnp.where(row == TAB_LSTART, lstart, 0.0)))
    before_ref[...] = before_ref[...] + count

    info = jnp.zeros((tm, LANE), F32)
    for k, col in ((INFO_GATE, gate1), (INFO_GATE + 1, gate2), (INFO_LPOS, lpos1), (INFO_LPOS + 1, lpos2)):
        info = jnp.where(lane == k, col, info)
    info_ref[0] = info


def _outproj(mixp, mixh, x, w_out, gain, wr_split, rbias, tm, n_groups, epg):
    B, S, D = x.shape
    n_sl = D // LANE
    lstrict = jnp.asarray(np.tril(np.ones((tm, tm), np.float32), -1), BF16)
    ustrict = jnp.asarray(np.triu(np.ones((LANE, LANE), np.float32), 1), BF16)
    const = lambda b, i: (0, 0)
    tile = lambda b, i: (b * (S // tm) + i, 0)
    return pl.pallas_call(
        functools.partial(_outproj_body, n_groups=n_groups, epg=epg),
        grid=(B, S // tm),
        in_specs=[
            pl.BlockSpec((1, mixp.shape[1], tm, LANE), lambda b, i: (b, 0, i, 0)),
            pl.BlockSpec((1, mixh.shape[1], tm, LANE), lambda b, i: (b, 0, i, 0)),
            pl.BlockSpec((1, tm, D), lambda b, i: (b, i, 0)),
            pl.BlockSpec(w_out.shape, const),
            pl.BlockSpec((1, D), const),
            pl.BlockSpec((D, 2 * LANE), const),
            pl.BlockSpec((1, LANE), const),
            pl.BlockSpec((tm, tm), const),
            pl.BlockSpec((LANE, LANE), const),
        ],
        out_specs=[
            pl.BlockSpec((1, tm, D), lambda b, i: (b, i, 0)),
            pl.BlockSpec((tm * n_sl, LANE), tile),
            pl.BlockSpec((1, tm, LANE), lambda b, i: (b, i, 0)),
            pl.BlockSpec((SUBLANE, LANE), tile),
        ],
        out_shape=[
            jax.ShapeDtypeStruct((B, S, D), F32),
            jax.ShapeDtypeStruct((B * S * n_sl, LANE), F32),
            jax.ShapeDtypeStruct((B, S, LANE), F32),
            jax.ShapeDtypeStruct((B * (S // tm) * SUBLANE, LANE), F32),
        ],
        scratch_shapes=[pltpu.VMEM((SUBLANE, LANE), F32)],
        compiler_params=_params(2),
        name="outproj",
    )(mixp, mixh, x, w_out, gain, wr_split, rbias, lstrict, ustrict)


def _dispatch_body(nloc_ref, pad_ref, p1_ref, p2_ref, dst_ref, xn_ref, xs_ref, buf_a, buf_b, zbuf, sem,
                   *, tm, n_sl, n_ch, n_fix):
    i = pl.program_id(0)
    last = pl.num_programs(0) - 1
    piece = MOE_CHUNK * n_sl
    stride = _index_stride(n_ch)

    def chunk_copy(buf, j, chunk, s):
        d = pl.multiple_of(chunk * piece, piece)
        return pltpu.make_async_copy(buf.at[pl.ds(j * piece, piece)], xs_ref.at[pl.ds(d, piece)], sem.at[s])

    def start(buf, half, s):
        n_loc = nloc_ref[2 * i + half]
        for j in range(n_ch):
            copy = chunk_copy(buf, j, dst_ref[half * stride + j], s)
            if j < n_fix:
                copy.start()
            else:
                pl.when(j < n_loc)(copy.start)

    def wait_chunks(buf, n, s):
        pltpu.make_async_copy(buf.at[pl.ds(0, n * piece)], xs_ref.at[pl.ds(0, n * piece)], sem.at[s]).wait()

    def wait(buf, tile, s):
        n_loc = nloc_ref[tile]
        wait_chunks(buf, n_fix, s)
        for j in range(n_fix, n_ch):
            pl.when(j < n_loc)(functools.partial(wait_chunks, buf, 1, s))

    def sort(buf, half):
        buf[...] = jnp.zeros_like(buf)

        def move(t, carry):
            row = xn_ref[pl.ds(pl.multiple_of((half * tm + t) * n_sl, n_sl), n_sl), :]
            buf[pl.ds(pl.multiple_of(p1_ref[half * tm + t] * n_sl, n_sl), n_sl), :] = row
            buf[pl.ds(pl.multiple_of(p2_ref[half * tm + t] * n_sl, n_sl), n_sl), :] = row
            return carry

        lax.fori_loop(0, tm, move, 0, unroll=8)

    @pl.when(i == 0)
    def _():
        zbuf[...] = jnp.zeros_like(zbuf)
        n_listed = pad_ref.shape[0] - 1
        n_chunks = xs_ref.shape[0] // piece

        def fill(k, carry):
            pl.when(pad_ref[k] >= 0)(chunk_copy(zbuf, 0, jnp.maximum(pad_ref[k], 0), 2).start)
            return carry

        def filled(k, carry):
            pl.when(pad_ref[k] >= 0)(functools.partial(wait_chunks, zbuf, 1, 2))
            return carry

        def fill_tail(c, carry):
            chunk_copy(zbuf, 0, c, 2).start()
            return carry

        def filled_tail(c, carry):
            wait_chunks(zbuf, 1, 2)
            return carry

        lax.fori_loop(0, n_listed, fill, 0)
        lax.fori_loop(pad_ref[n_listed], n_chunks, fill_tail, 0)
        lax.fori_loop(0, n_listed, filled, 0)
        lax.fori_loop(pad_ref[n_listed], n_chunks, filled_tail, 0)

    @pl.when(i > 0)
    def _():
        wait(buf_a, 2 * i - 2, 0)

    sort(buf_a, 0)
    start(buf_a, 0, 0)

    @pl.when(i > 0)
    def _():
        wait(buf_b, 2 * i - 1, 1)

    sort(buf_b, 1)
    start(buf_b, 1, 1)

    @pl.when(i == last)
    def _():
        wait(buf_a, 2 * i, 0)
        wait(buf_b, 2 * i + 1, 1)


def _dispatch(n_local, pad_chunks, lpos1, lpos2, chunk_dst, xn_rows, n_slot_chunks, tm, cap, n_sl):
    n_tiles = lpos1.shape[0] // tm
    assert n_tiles % 2 == 0
    n_ch = cap // MOE_CHUNK
    stride = _index_stride(n_ch)
    piece = MOE_CHUNK * n_sl
    grid_spec = pltpu.PrefetchScalarGridSpec(
        num_scalar_prefetch=2,
        grid=(n_tiles // 2,),
        in_specs=[
            pl.BlockSpec((2 * tm,), lambda i, nl, pc: (i,), memory_space=pltpu.SMEM),
            pl.BlockSpec((2 * tm,), lambda i, nl, pc: (i,), memory_space=pltpu.SMEM),
            pl.BlockSpec((2 * stride,), lambda i, nl, pc: (i,), memory_space=pltpu.SMEM),
            pl.BlockSpec((2 * tm * n_sl, LANE), lambda i, nl, pc: (i, 0)),
        ],
        out_specs=pl.BlockSpec(memory_space=pl.ANY),
        scratch_shapes=[pltpu.VMEM((cap * n_sl, LANE), F32), pltpu.VMEM((cap * n_sl, LANE), F32),
                        pltpu.VMEM((piece, LANE), F32), pltpu.SemaphoreType.DMA((3,))],
    )
    return pl.pallas_call(
        functools.partial(_dispatch_body, tm=tm, n_sl=n_sl, n_ch=n_ch, n_fix=TOP_K * tm // MOE_CHUNK),
        grid_spec=grid_spec,
        out_shape=jax.ShapeDtypeStruct((n_slot_chunks * piece, LANE), F32),
        compiler_params=_params(1),
        name="dispatch",
    )(n_local, pad_chunks, lpos1, lpos2, _pad_lists(chunk_dst, n_ch), xn_rows)


def _start_rows(idx_ref, first, n, src_ref, dst_ref, sem, rows_per):
    for r in range(n):
        s = pl.multiple_of(idx_ref[first + r] * rows_per, rows_per)
        pltpu.make_async_copy(src_ref.at[pl.ds(s, rows_per)], dst_ref.at[pl.ds(r * rows_per, rows_per)], sem).start()


def _wait_rows(n, src_ref, dst_ref, sem, rows_per):
    total = n * rows_per
    pltpu.make_async_copy(src_ref.at[pl.ds(0, total)], dst_ref.at[pl.ds(0, total)], sem).wait()


def _index_stride(n):
    return max(LANE, pl.next_power_of_2(n))


def _pad_lists(idx, n):
    return jnp.pad(idx.reshape(-1, n), ((0, 0), (0, _index_stride(n) - n))).reshape(-1)


def _slab_rows(ref, n, n_sl, first=0):
    return jnp.concatenate([ref[pl.ds(first + j, n, stride=n_sl), :] for j in range(n_sl)], axis=-1)


def _experts_body(bexp_ref, nused_ref, xs_ref, *refs, n_sl):
    i = pl.program_id(0)
    blk = EXPERT_ROWS
    y_ref = refs[-1]

    def mlp(w1_ref, w3_ref, w2_ref, part):
        first = part * blk * n_sl
        xb = _slab_rows(xs_ref, blk, n_sl, first).astype(BF16)
        a = jnp.dot(xb, w1_ref[0], preferred_element_type=F32)
        h = (a * jax.nn.sigmoid(a)) * jnp.dot(xb, w3_ref[0], preferred_element_type=F32)
        y = jnp.dot(h.astype(BF16), w2_ref[0], preferred_element_type=F32)
        for j in range(n_sl):
            y_ref[pl.ds(first + j, blk, stride=n_sl), :] = y[:, j * LANE:(j + 1) * LANE]

    @pl.when(EXPERT_STEP * i < nused_ref[0])
    def _():
        for part in range(EXPERT_STEP):
            mlp(*refs[3 * part:3 * part + 3], part)

    @pl.when(EXPERT_STEP * i >= nused_ref[0])
    def _():
        y_ref[...] = jnp.zeros_like(y_ref)


def _experts(block_expert, n_used, xs_rows, w1, w3, w2, n_sl):
    n_blocks = block_expert.shape[0]
    g = EXPERT_STEP
    assert n_blocks % g == 0
    blk = EXPERT_ROWS
    _, D, DE = w1.shape

    def wspec(shape, part):
        return pl.BlockSpec(shape, lambda i, be, nu, part=part: (be[g * i + part], 0, 0))

    weights = [spec for part in range(g) for spec in (wspec((1, D, DE), part), wspec((1, D, DE), part),
                                                      wspec((1, DE, D), part))]
    grid_spec = pltpu.PrefetchScalarGridSpec(
        num_scalar_prefetch=2,
        grid=(n_blocks // g,),
        in_specs=[
            pl.BlockSpec((g * blk * n_sl, LANE), lambda i, be, nu: (jnp.minimum(i, nu[0] // g - 1), 0)),
        ] + weights,
        out_specs=pl.BlockSpec((g * blk * n_sl, LANE), lambda i, be, nu: (i, 0)),
    )
    return pl.pallas_call(
        functools.partial(_experts_body, n_sl=n_sl),
        grid_spec=grid_spec,
        out_shape=jax.ShapeDtypeStruct((n_blocks * blk * n_sl, LANE), F32),
        compiler_params=_params(1),
        name="experts",
    )(block_expert, n_used, xs_rows, *([w1, w3, w2] * g))


def _final_body(p1_ref, p2_ref, src_ref, nxt_ref, x2_ref, info_ref, gain_ref, y_ref, o_ref,
                ybuf_a, ybuf_b, tok1, tok2, sem, *, n_sl, tm, n_ch):
    i = pl.program_id(0)
    piece = MOE_CHUNK * n_sl

    def combine(ybuf, half):
        def unsort(t, carry):
            dst = pl.ds(pl.multiple_of(t * n_sl, n_sl), n_sl)
            tok1[dst, :] = ybuf[pl.ds(pl.multiple_of(p1_ref[half * tm + t] * n_sl, n_sl), n_sl), :]
            tok2[dst, :] = ybuf[pl.ds(pl.multiple_of(p2_ref[half * tm + t] * n_sl, n_sl), n_sl), :]
            return carry

        lax.fori_loop(0, tm, unsort, 0, unroll=8)
        rows = pl.ds(half * tm, tm)
        info = info_ref[rows, :]
        moe = (_slab_rows(tok1, tm, n_sl) * info[:, INFO_GATE:INFO_GATE + 1]
               + _slab_rows(tok2, tm, n_sl) * info[:, INFO_GATE + 1:INFO_GATE + 2])
        o_ref[rows, :] = _rms(x2_ref[rows, :] + moe, gain_ref[...])

    @pl.when(i == 0)
    def _():
        _start_rows(src_ref, 0, n_ch, y_ref, ybuf_a, sem.at[0], piece)

    _start_rows(src_ref, _index_stride(n_ch), n_ch, y_ref, ybuf_b, sem.at[1], piece)
    _wait_rows(n_ch, y_ref, ybuf_a, sem.at[0], piece)
    combine(ybuf_a, 0)
    _start_rows(nxt_ref, 0, n_ch, y_ref, ybuf_a, sem.at[0], piece)
    _wait_rows(n_ch, y_ref, ybuf_b, sem.at[1], piece)
    combine(ybuf_b, 1)

    @pl.when(i == pl.num_programs(0) - 1)
    def _():
        _wait_rows(n_ch, y_ref, ybuf_a, sem.at[0], piece)


def _final(lpos1, lpos2, y_src, x2, info, gain, y_rows, tm, cap, n_sl):
    T, D = x2.shape
    assert T % (2 * tm) == 0
    steps = T // (2 * tm)
    n_ch = cap // MOE_CHUNK
    stride = _index_stride(n_ch)
    y_src = _pad_lists(y_src, n_ch)
    cur = lambda i: (i,)
    nxt = lambda i: (jnp.minimum(2 * i + 2, 2 * steps - 1),)
    ybuf = pltpu.VMEM((cap * n_sl, LANE), F32)
    tbuf = pltpu.VMEM((tm * n_sl, LANE), F32)
    return pl.pallas_call(
        functools.partial(_final_body, n_sl=n_sl, tm=tm, n_ch=n_ch),
        grid=(steps,),
        in_specs=[
            pl.BlockSpec((2 * tm,), cur, memory_space=pltpu.SMEM),
            pl.BlockSpec((2 * tm,), cur, memory_space=pltpu.SMEM),
            pl.BlockSpec((2 * stride,), cur, memory_space=pltpu.SMEM),
            pl.BlockSpec((stride,), nxt, memory_space=pltpu.SMEM),
            pl.BlockSpec((2 * tm, D), lambda i: (i, 0)),
            pl.BlockSpec((2 * tm, LANE), lambda i: (i, 0)),
            pl.BlockSpec((1, D), lambda i: (0, 0)),
            pl.BlockSpec(memory_space=pl.ANY),
        ],
        out_specs=pl.BlockSpec((2 * tm, D), lambda i: (i, 0)),
        out_shape=jax.ShapeDtypeStruct((T, D), F32),
        scratch_shapes=[ybuf, ybuf, tbuf, tbuf, pltpu.SemaphoreType.DMA((2,))],
        compiler_params=_params(1),
        name="final",
    )(lpos1, lpos2, y_src, y_src, x2, info, gain, y_rows)


def _tile(n, pref):
    return pref if n % pref == 0 else n


def _layer(x, p):
    B, S, D = x.shape
    T = B * S
    n_sl = D // LANE
    G = p["pool_w"].shape[0]
    E = p["w1"].shape[0]
    n_groups = p["n_groups"]
    tm = _tile(S, 512)
    n_tiles = T // tm

    u = _inproj(x, p["norm_mix"], p["w_in"], tm)
    mixp = _pool(u, p["pool_w"], p["pool_scale"])
    mixh = _hgrn(u, p["lb_f"], p["lb_b"], p["hg_gain"], G)
    x2, xn_rows, info, tab = _outproj(mixp, mixh, x, p["w_out"], p["norm_ffn"], p["wr_split"],
                                      p["rbias"], tm, n_groups, E // n_groups)

    blk = EXPERT_ROWS
    ch = MOE_CHUNK
    cap = TOP_K * tm + E * ch
    cap_ch = cap // ch
    n_blocks = -(-(T * TOP_K + n_tiles * E * (ch - 1)) // blk) + E + EXPERT_STEP - 1
    n_blocks += -n_blocks % EXPERT_STEP
    n_slot_chunks = n_blocks * blk // ch
    tab = tab.reshape(n_tiles, SUBLANE, LANE)[:, :, n_groups:n_groups + E].astype(jnp.int32)
    count, before, lstart = tab[:, TAB_COUNT], tab[:, TAB_BEFORE], tab[:, TAB_LSTART]
    total = before[-1] + count[-1]
    padded = (total + blk - 1) // blk * blk
    padded = padded.at[E - 1].add(blk * (-(jnp.sum(padded) // blk) % EXPERT_STEP))
    pend = jnp.cumsum(padded)
    pstart = pend - padded
    first_slot = jnp.arange(n_blocks, dtype=jnp.int32) * blk
    block_expert = jnp.minimum(jnp.sum(pend[None, :] <= first_slot[:, None], axis=1), E - 1).astype(jnp.int32)
    n_used = (pend[-1:] // blk).astype(jnp.int32)

    slot_chunk = (pstart[None, :] + before) // ch
    local_chunk = lstart // ch
    n_local = (lstart[:, -1] + count[:, -1]) // ch
    off = slot_chunk - local_chunk
    step = off - jnp.concatenate([jnp.zeros((n_tiles, 1), jnp.int32), off[:, :-1]], axis=1)
    steps = jnp.zeros((n_tiles, cap_ch + 1), jnp.int32).at[jnp.arange(n_tiles)[:, None], local_chunk].add(step)
    chunk_slot = jnp.arange(cap_ch, dtype=jnp.int32)[None, :] + jnp.cumsum(steps, axis=1)[:, :cap_ch]
    chunk_slot = jnp.clip(chunk_slot, 0, n_slot_chunks - 1).reshape(-1)
    k = jnp.arange(EXPERT_STEP * blk // ch, dtype=jnp.int32)[None, :]
    pad_chunks = ((pstart + total) // ch)[:, None] + k
    pad_chunks = jnp.where(pad_chunks < (pend // ch)[:, None], pad_chunks, -1).reshape(-1)
    pad_chunks = jnp.concatenate([pad_chunks, pend[-1:] // ch])

    info2 = info.reshape(T, LANE)
    lpos = info2[:, INFO_LPOS:INFO_LPOS + TOP_K].astype(jnp.int32)
    xs_rows = _dispatch(n_local, pad_chunks, lpos[:, 0], lpos[:, 1], chunk_slot, xn_rows, n_slot_chunks, tm, cap, n_sl)
    y_rows = _experts(block_expert, n_used, xs_rows, p["w1"], p["w3"], p["w2"], n_sl)
    out = _final(lpos[:, 0], lpos[:, 1], chunk_slot, x2.reshape(T, D), info2, p["norm_final"], y_rows, tm, cap, n_sl)
    return out.reshape(B, S, D)


def kernel(x_prompt, x_sample, w_in, w_out, pool_w, pool_scale, hg_lb_fwd, hg_lb_bwd, hg_norm_gain, norm_mix, norm_ffn, router_group_w, router_group_b, router_expert_w, router_expert_b, expert_w1, expert_w3, expert_w2, norm_final):
    depth = w_in.shape[0]
    assert depth == 1, "the final norm is fused into the last layer's combine kernel; one layer supported"
    D = w_in.shape[1]
    hg_width = hg_lb_fwd.shape[1]
    dv = hg_norm_gain.shape[1]
    pg = pool_w.shape[2]
    assert dv == LANE and pg == LANE and D % LANE == 0
    H = hg_width // dv
    n_groups = router_group_w.shape[-1]
    E = router_expert_w.shape[-1]
    assert n_groups + E <= LANE

    lb_f = jnp.cumsum(jax.nn.softmax(hg_lb_fwd.astype(F32), axis=0), axis=0)
    lb_b = jnp.cumsum(jax.nn.softmax(hg_lb_bwd.astype(F32), axis=0), axis=0)
    l = 0
    wr = jnp.concatenate([router_group_w[l], router_expert_w[l]], axis=1).astype(F32)
    wr = jnp.pad(wr, ((0, 0), (0, LANE - wr.shape[1])))
    wr_hi = wr.astype(BF16)
    rbias = jnp.concatenate([router_group_b[l], router_expert_b[l]]).astype(F32)
    p = dict(
        n_groups=n_groups,
        w_in=w_in[l].astype(BF16), w_out=w_out[l].astype(BF16),
        pool_w=pool_w[l], pool_scale=pool_scale[l],
        lb_f=lb_f[l].reshape(H, 1, dv), lb_b=lb_b[l].reshape(H, 1, dv), hg_gain=hg_norm_gain[l].reshape(1, dv).astype(F32),
        norm_mix=norm_mix[l].reshape(1, D).astype(F32), norm_ffn=norm_ffn[l].reshape(1, D).astype(F32),
        norm_final=norm_final.reshape(1, D).astype(F32),
        wr_split=jnp.concatenate([wr_hi, (wr - wr_hi.astype(F32)).astype(BF16)], axis=1),
        rbias=jnp.pad(rbias, (0, LANE - rbias.shape[0])).reshape(1, LANE),
        w1=expert_w1[l].astype(BF16), w3=expert_w3[l].astype(BF16), w2=expert_w2[l].astype(BF16),
    )
    return (_layer(x_prompt, p), _layer(x_sample, p))
```
